```python
import math, functools
import jax, jax.numpy as jnp
from jax import lax
import numpy as np

D_MODEL = 1024
BATCH = 4
SEQ = 4096
DEPTH = 1
DEC_BATCH = 128
DEC_SEQ = 1
PAST_LEN = 8192
PAGE_SIZE = 128

RET_HEADS = 4
RET_DK = 128
RET_DV = 256
RET_QK = RET_HEADS * RET_DK
RET_V = RET_HEADS * RET_DV
RET_CHUNK = 128
SWA_HEADS = 16
SWA_KV_HEADS = 4
SWA_GROUP = SWA_HEADS // SWA_KV_HEADS
SWA_HD = 64
SWA_Q = SWA_HEADS * SWA_HD
SWA_KV = SWA_KV_HEADS * SWA_HD
WINDOW = 128
SWA_BLOCK = WINDOW
ROPE_THETA = 10000.0
EPS = 1e-6
IN_WIDTH = 2 * RET_QK + 2 * RET_V + 2 * SWA_Q + 2 * SWA_KV + 2 * D_MODEL

kernel_name = 'retnet_swa_sink_gated_hybrid_step'


def _split_points():
    widths = (RET_QK, RET_QK, RET_V, RET_V, SWA_Q, SWA_KV, SWA_KV, SWA_Q, D_MODEL, D_MODEL)
    pts, acc = [], 0
    for w in widths[:-1]:
        acc += w
        pts.append(acc)
    return pts


def rms_norm(x, g):
    xf = x.astype(jnp.float32)
    y = xf * lax.rsqrt(jnp.mean(xf * xf, axis=-1, keepdims=True) + EPS) * g.astype(jnp.float32)
    return y.astype(x.dtype)


def rope(x, pos):
    d = x.shape[-1]
    inv = ROPE_THETA ** (-jnp.arange(0, d, 2, dtype=jnp.float32) / d)
    ang = pos.astype(jnp.float32)[:, None] * inv[None, :]
    c = jnp.cos(ang)[:, None, :]
    s = jnp.sin(ang)[:, None, :]
    xf = x.astype(jnp.float32)
    x1, x2 = xf[..., : d // 2], xf[..., d // 2:]
    return jnp.concatenate([x1 * c - x2 * s, x2 * c + x1 * s], axis=-1).astype(x.dtype)


def ret_log_decay():
    return jnp.log(1.0 - 2.0 ** (-5.0 - jnp.arange(RET_HEADS, dtype=jnp.float32)))


def retention_chunk(S, q, k, v, lg):
    L = q.shape[1]
    n = jnp.arange(L, dtype=jnp.float32)
    diff = n[:, None] - n[None, :]
    dmask = jnp.where(diff[None] >= 0, jnp.exp(jnp.maximum(diff, 0.0)[None] * lg[:, None, None]), 0.0)
    scores = jnp.einsum('bnhd,bmhd->bhnm', q, k) * dmask[None]
    o = jnp.einsum('bhnm,bmhe->bnhe', scores, v)
    cross = jnp.exp((n[:, None] + 1.0) * lg[None, :])
    o = o + jnp.einsum('bnhd,bhde->bnhe', q, S) * cross[None, :, :, None]
    k_dec = jnp.exp((L - 1.0 - n)[:, None] * lg[None, :])
    S_new = jnp.exp(L * lg)[None, :, None, None] * S + jnp.einsum('bmhd,bmhe,mh->bhde', k, v, k_dec)
    return S_new, o


def retention_prompt(q, k, v, lg):
    B, T = q.shape[:2]
    nc = T // RET_CHUNK

    def to_chunks(a):
        return a.reshape((B, nc, RET_CHUNK) + a.shape[2:]).swapaxes(0, 1)

    S0 = jnp.zeros((B, RET_HEADS, RET_DK, RET_DV), jnp.float32)
    S, o = lax.scan(lambda s, c: retention_chunk(s, c[0], c[1], c[2], lg), S0,
                    (to_chunks(q), to_chunks(k), to_chunks(v)))
    o = o.swapaxes(0, 1).reshape(B, T, RET_HEADS, RET_DV)
    return o, S


def retention_step(q, k, v, state, lg):
    S_new, o = retention_chunk(state.astype(jnp.float32), q, k, v, lg)
    return o, S_new


def sink_attention(q, k, v, allowed, sinks):
    s = jnp.einsum('bnqhgd,bnkhd->bnhgqk', q.astype(jnp.float32), k.astype(jnp.float32)) * (SWA_HD ** -0.5)
    s = jnp.where(allowed[None, :, None, None], s, -jnp.inf)
    sink = jnp.broadcast_to(sinks.astype(jnp.float32)[None, None, :, :, None, None], s.shape[:-1] + (1,))
    p = jax.nn.softmax(jnp.concatenate([s, sink], axis=-1), axis=-1)[..., :-1]
    return jnp.einsum('bnhgqk,bnkhd->bnqhgd', p, v.astype(jnp.float32))


def swa_prompt(q, k, v, sinks):
    B, T = q.shape[:2]
    nb = T // SWA_BLOCK
    qb = q.reshape(B, nb, SWA_BLOCK, SWA_KV_HEADS, SWA_GROUP, SWA_HD)
    kb = k.reshape(B, nb, SWA_BLOCK, SWA_KV_HEADS, SWA_HD)
    vb = v.reshape(B, nb, SWA_BLOCK, SWA_KV_HEADS, SWA_HD)

    def with_prev(a):
        prev = jnp.pad(a[:, :-1], ((0, 0), (1, 0), (0, 0), (0, 0), (0, 0)))
        return jnp.concatenate([prev, a], axis=2)

    i = jnp.arange(SWA_BLOCK)[:, None]
    j = jnp.arange(2 * SWA_BLOCK)[None, :]
    d = i + SWA_BLOCK - j
    band = (d >= 0) & (d < WINDOW)
    allowed = band[None] & ((jnp.arange(nb)[:, None, None] > 0) | (j[None] >= SWA_BLOCK))
    o = sink_attention(qb, with_prev(kb), with_prev(vb), allowed, sinks)
    o = o.reshape(B, T, SWA_KV_HEADS, SWA_GROUP, SWA_HD)
    keep = min(WINDOW, T)
    return o, k[:, T - keep:], v[:, T - keep:]


def swa_sample(q, k, v, sinks, cache_k, cache_v):
    Ts = q.shape[1]
    Wc = cache_k.shape[1]
    kk = jnp.concatenate([cache_k.astype(k.dtype), k], axis=1)
    vv = jnp.concatenate([cache_v.astype(v.dtype), v], axis=1)
    j = jnp.arange(Ts)[:, None]
    m = jnp.arange(Wc + Ts)[None, :]
    d = Wc + j - m
    allowed = ((d >= 0) & (d < WINDOW))[None]
    o = sink_attention(q[:, None], kk[:, None], vv[:, None], allowed, sinks)[:, 0]
    return o, kk[:, Ts:], vv[:, Ts:]


def hybrid_layer(x, pos, retention_fn, swa_fn, norm_g, w_in, ret_norm_g, swa_q_g, swa_k_g,
                 swa_sinks, w_br_ret, w_br_swa, w_out):
    B, T, _ = x.shape
    h = rms_norm(x, norm_g)
    proj = jnp.einsum('btd,de->bte', h, w_in)
    rq, rk, rv, rg, sq, sk, sv, sg, mg_r, mg_s = jnp.split(proj, _split_points(), axis=-1)
    rq = rope(rq.reshape(B, T, RET_HEADS, RET_DK), pos).astype(jnp.float32)
    rk = rope(rk.reshape(B, T, RET_HEADS, RET_DK), pos).astype(jnp.float32) * (RET_DK ** -0.5)
    rv = rv.reshape(B, T, RET_HEADS, RET_DV).astype(jnp.float32)
    o_r, ret_state = retention_fn(rq, rk, rv)
    o_r = rms_norm(o_r, ret_norm_g).reshape(B, T, RET_V).astype(x.dtype) * jax.nn.silu(rg)
    br_r = jnp.einsum('bte,ed->btd', o_r, w_br_ret)
    sq = rope(rms_norm(sq.reshape(B, T, SWA_HEADS, SWA_HD), swa_q_g), pos)
    sk = rope(rms_norm(sk.reshape(B, T, SWA_KV_HEADS, SWA_HD), swa_k_g), pos)
    sv = sv.reshape(B, T, SWA_KV_HEADS, SWA_HD)
    o_s, k_buf, v_buf = swa_fn(sq.reshape(B, T, SWA_KV_HEADS, SWA_GROUP, SWA_HD), sk, sv,
                               swa_sinks.reshape(SWA_KV_HEADS, SWA_GROUP))
    o_s = o_s.reshape(B, T, SWA_Q).astype(x.dtype) * jax.nn.silu(sg)
    br_s = jnp.einsum('bte,ed->btd', o_s, w_br_swa)
    merged = jax.nn.sigmoid(mg_r) * br_r + jax.nn.sigmoid(mg_s) * br_s
    y = x + jnp.einsum('btd,de->bte', merged, w_out)
    return y, ret_state, k_buf, v_buf


def setup_inputs(seed: int = 0) -> dict:
    key = jax.random.key(seed)
    ks = jax.random.split(key, 14)
    nrm = jax.random.normal
    f32 = jnp.float32
    swa_cache = min(WINDOW, PAST_LEN)
    return {
        'x_prompt': nrm(ks[0], (BATCH, SEQ, D_MODEL), f32),
        'x_sample': nrm(ks[1], (DEC_BATCH, DEC_SEQ, D_MODEL), f32),
        'state_ret': 0.05 * nrm(ks[2], (DEPTH, DEC_BATCH, RET_HEADS, RET_DK, RET_DV), f32),
        'cache_swa_k': nrm(ks[3], (DEPTH, DEC_BATCH, swa_cache, SWA_KV_HEADS, SWA_HD), f32),
        'cache_swa_v': nrm(ks[4], (DEPTH, DEC_BATCH, swa_cache, SWA_KV_HEADS, SWA_HD), f32),
        'norm_g': 1.0 + 0.02 * nrm(ks[5], (DEPTH, D_MODEL), f32),
        'w_in': nrm(ks[6], (DEPTH, D_MODEL, IN_WIDTH), f32) * D_MODEL ** -0.5,
        'ret_norm_g': 1.0 + 0.02 * nrm(ks[7], (DEPTH, RET_HEADS, RET_DV), f32),
        'swa_q_g': 1.0 + 0.02 * nrm(ks[8], (DEPTH, SWA_HD), f32),
        'swa_k_g': 1.0 + 0.02 * nrm(ks[9], (DEPTH, SWA_HD), f32),
        'swa_sinks': 0.5 * nrm(ks[10], (DEPTH, SWA_HEADS), f32),
        'w_br_ret': nrm(ks[11], (DEPTH, RET_V, D_MODEL), f32) * RET_V ** -0.5,
        'w_br_swa': nrm(ks[12], (DEPTH, SWA_Q, D_MODEL), f32) * SWA_Q ** -0.5,
        'w_out': nrm(ks[13], (DEPTH, D_MODEL, D_MODEL), f32) * D_MODEL ** -0.5,
    }


def reference(x_prompt, x_sample, state_ret, cache_swa_k, cache_swa_v, norm_g, w_in, ret_norm_g,
              swa_q_g, swa_k_g, swa_sinks, w_br_ret, w_br_swa, w_out):
    lg = ret_log_decay()
    pos_p = jnp.arange(x_prompt.shape[1], dtype=jnp.int32)
    pos_s = PAST_LEN + jnp.arange(x_sample.shape[1], dtype=jnp.int32)
    yp, ys = x_prompt, x_sample
    rp_l, rs_l, kp_l, vp_l, ksm_l, vsm_l = [], [], [], [], [], []
    for l in range(DEPTH):
        weights = (norm_g[l], w_in[l], ret_norm_g[l], swa_q_g[l], swa_k_g[l], swa_sinks[l],
                   w_br_ret[l], w_br_swa[l], w_out[l])
        yp, rp, kp, vp = hybrid_layer(yp, pos_p, functools.partial(retention_prompt, lg=lg),
                                      swa_prompt, *weights)
        ys, rs, ksm, vsm = hybrid_layer(ys, pos_s,
                                        functools.partial(retention_step, state=state_ret[l], lg=lg),
                                        functools.partial(swa_sample, cache_k=cache_swa_k[l], cache_v=cache_swa_v[l]),
                                        *weights)
        rp_l.append(rp); rs_l.append(rs); kp_l.append(kp); vp_l.append(vp); ksm_l.append(ksm); vsm_l.append(vsm)
    return (yp, ys, jnp.stack(rp_l), jnp.stack(rs_l), jnp.stack(kp_l), jnp.stack(vp_l),
            jnp.stack(ksm_l), jnp.stack(vsm_l))
```

```python
import functools
import math

import numpy as np
import jax
import jax.numpy as jnp
from jax import lax
from jax.experimental import pallas as pl
from jax.experimental.pallas import tpu as pltpu

F32 = jnp.float32
BF16 = jnp.bfloat16

D_MODEL = 1024
RET_HEADS = 4
RET_DK = 128
RET_DV = 256
RET_QK = RET_HEADS * RET_DK
RET_V = RET_HEADS * RET_DV
SWA_HEADS = 16
SWA_KV_HEADS = 4
SWA_HD = 64
SWA_Q = SWA_HEADS * SWA_HD
SWA_KV = SWA_KV_HEADS * SWA_HD
WINDOW = 128
ROPE_THETA = 10000.0
EPS = 1e-6
PAST_LEN = 8192

C_RQ = 0
C_RK = C_RQ + RET_QK
C_RV = C_RK + RET_QK
C_RG = C_RV + RET_V
C_SQ = C_RG + RET_V
C_SK = C_SQ + SWA_Q
C_SV = C_SK + SWA_KV
C_SG = C_SV + SWA_KV
C_MR = C_SG + SWA_Q
C_MS = C_MR + D_MODEL
IN_WIDTH = C_MS + D_MODEL

LANES = 128
PROMPT_TM = 256
DEC_BS = 8
VMEM_LIMIT = 56 * 1024 * 1024


def _dot(a, b):
    return jnp.dot(a, b, preferred_element_type=F32)


def _dot_nt(a, b):
    return lax.dot_general(a, b, (((1,), (1,)), ((), ())), preferred_element_type=F32)


def _dot_tn(a, b):
    return lax.dot_general(a, b, (((0,), (0,)), ((), ())), preferred_element_type=F32)


def _sigmoid(x):
    return 1.0 / (1.0 + jnp.exp(-x))


def _rms_rows(x, g):
    return x * lax.rsqrt(jnp.mean(x * x, axis=-1, keepdims=True) + EPS) * g


def _group_mean_sq(x, gmat):
    x2 = x * x
    hi = x2.astype(BF16)
    lo = (x2 - hi.astype(F32)).astype(BF16)
    return _dot(hi, gmat) + _dot(lo, gmat)


def _rope128(x, cos, sin_signed):
    return x * cos + pltpu.roll(x, 64, 1) * sin_signed


def _rope64(x, cos, sin_signed, first_half):
    rot = jnp.where(first_half, pltpu.roll(x, 96, 1), pltpu.roll(x, 32, 1))
    return x * cos + rot * sin_signed


def _swa_norm_rope(x, gmat, gain, cos, sin_signed, first_half):
    xn = x * lax.rsqrt(_group_mean_sq(x, gmat) + EPS) * gain
    cols = [_rope64(xn[:, c * LANES:(c + 1) * LANES], cos, sin_signed, first_half) for c in range(2)]
    return jnp.concatenate(cols, axis=1)


def _prompt_kernel(x_ref, ng_ref, win_ref, rng_ref, qg_ref, kg_ref, sink_ref, wbr_ref, wbs_ref, wo_ref,
                   rcos_ref, rsin_ref, scos_ref, ssin_ref, dmask_ref, cross_ref, kdec_ref, gmat_ref,
                   y_ref, st_ref, ko_ref, vo_ref,
                   s_scr, pk_scr, pv_scr, or_scr, os_scr, *, decay_chunk):
    t = pl.program_id(1)
    tm = x_ref.shape[0]
    nblk = tm // WINDOW

    @pl.when(t == 0)
    def _():
        s_scr[...] = jnp.zeros_like(s_scr)
        pk_scr[...] = jnp.zeros_like(pk_scr)
        pv_scr[...] = jnp.zeros_like(pv_scr)

    x = x_ref[...]
    h = _rms_rows(x, ng_ref[...]).astype(BF16)

    def proj(lo, hi):
        return _dot(h, win_ref[:, lo:hi])

    rq = proj(C_RQ, C_RK)
    rk = proj(C_RK, C_RV)
    rv = proj(C_RV, C_RG).astype(BF16)
    rg = proj(C_RG, C_SQ)
    rcos = rcos_ref[...]
    rsin = rsin_ref[...]
    k_scale = RET_DK ** -0.5
    for hd in range(RET_HEADS):
        q = _rope128(rq[:, hd * RET_DK:(hd + 1) * RET_DK], rcos, rsin)
        k = _rope128(rk[:, hd * RET_DK:(hd + 1) * RET_DK], rcos, rsin) * k_scale
        v = rv[:, hd * RET_DV:(hd + 1) * RET_DV]
        state = s_scr[hd]
        scores = _dot_nt(q.astype(BF16), k.astype(BF16)) * dmask_ref[hd]
        o = _dot(scores.astype(BF16), v) + _dot((q * cross_ref[hd]).astype(BF16), state.astype(BF16))
        s_scr[hd] = decay_chunk[hd] * state + _dot_tn((k * kdec_ref[hd]).astype(BF16), v)
        g = rg[:, hd * RET_DV:(hd + 1) * RET_DV]
        on = _rms_rows(o, rng_ref[:, hd * RET_DV:(hd + 1) * RET_DV])
        or_scr[:, hd * RET_DV:(hd + 1) * RET_DV] = (on * (g * _sigmoid(g))).astype(BF16)

    gmat = gmat_ref[...]
    scos = scos_ref[...]
    ssin = ssin_ref[...]
    lane = lax.broadcasted_iota(jnp.int32, (1, LANES), 1)
    first_half = (lane % SWA_HD) < (SWA_HD // 2)
    low = lane < SWA_HD
    qg = qg_ref[...]
    sq_cols = []
    for cb in range(SWA_Q // 256):
        sq_cols.append(_swa_norm_rope(proj(C_SQ + cb * 256, C_SQ + (cb + 1) * 256), gmat, qg, scos, ssin,
                                      first_half) * (SWA_HD ** -0.5))
    sk = _swa_norm_rope(proj(C_SK, C_SV), gmat, kg_ref[...], scos, ssin, first_half)
    sv = proj(C_SV, C_SG)
    sg = proj(C_SG, C_MR)

    row_i = lax.broadcasted_iota(jnp.int32, (WINDOW, 2 * WINDOW), 0)
    col_j = lax.broadcasted_iota(jnp.int32, (WINDOW, 2 * WINDOW), 1)
    for c in range(nblk):
        r0 = c * WINDOW
        k_cur = sk[r0:r0 + WINDOW]
        v_cur = sv[r0:r0 + WINDOW]
        k_prev = pk_scr[...] if c == 0 else sk[r0 - WINDOW:r0]
        v_prev = pv_scr[...] if c == 0 else sv[r0 - WINDOW:r0]
        kcat = jnp.concatenate([k_prev, k_cur], axis=0)
        vcat = jnp.concatenate([v_prev, v_cur], axis=0)
        lower = row_i + 1
        if c == 0:
            lower = jnp.maximum(lower, jnp.where(t == 0, WINDOW, 0))
        allowed = (col_j >= lower) & (col_j <= row_i + WINDOW)
        for j in range(SWA_KV_HEADS):
            col, half = divmod(j, 2)
            k2 = kcat[:, col * LANES:(col + 1) * LANES]
            v2 = vcat[:, col * LANES:(col + 1) * LANES]
            k2r = pltpu.roll(k2, SWA_HD, 1)
            v2r = pltpu.roll(v2, SWA_HD, 1)
            if half == 0:
                kbd = jnp.concatenate([jnp.where(low, k2, 0.0), jnp.where(low, 0.0, k2r)], axis=0)
                vbd = jnp.concatenate([jnp.where(low, v2, 0.0), jnp.where(low, 0.0, v2r)], axis=0)
            else:
                kbd = jnp.concatenate([jnp.where(low, k2r, 0.0), jnp.where(low, 0.0, k2)], axis=0)
                vbd = jnp.concatenate([jnp.where(low, v2r, 0.0), jnp.where(low, 0.0, v2)], axis=0)
            kbd = kbd.astype(BF16)
            vbd = vbd.astype(BF16)
            for p in (2 * j, 2 * j + 1):
                qp = sq_cols[p // 2][r0:r0 + WINDOW, (p % 2) * LANES:(p % 2 + 1) * LANES].astype(BF16)
                s2 = _dot_nt(qp, kbd)
                es, invs = [], []
                for a in range(2):
                    sink = sink_ref[2 * p + a]
                    s = jnp.where(allowed, s2[:, a * 2 * WINDOW:(a + 1) * 2 * WINDOW], -jnp.inf)
                    m = jnp.maximum(jnp.max(s, axis=-1, keepdims=True), sink)
                    e = jnp.exp(s - m)
                    invs.append(1.0 / (jnp.sum(e, axis=-1, keepdims=True) + jnp.exp(sink - m)))
                    es.append(e.astype(BF16))
                o2 = _dot(jnp.concatenate(es, axis=1), vbd)
                o2 = o2 * jnp.where(low, invs[0], invs[1])
                g = sg[r0:r0 + WINDOW, p * LANES:(p + 1) * LANES]
                os_scr[r0:r0 + WINDOW, p * LANES:(p + 1) * LANES] = (o2 * (g * _sigmoid(g))).astype(BF16)
    pk_scr[...] = sk[tm - WINDOW:]
    pv_scr[...] = sv[tm - WINDOW:]

    br_r = _dot(or_scr[...], wbr_ref[...])
    br_s = _dot(os_scr[...], wbs_ref[...])
    merged = _sigmoid(proj(C_MR, C_MS)) * br_r + _sigmoid(proj(C_MS, IN_WIDTH)) * br_s
    y_ref[...] = x + _dot(merged.astype(BF16), wo_ref[...])

    @pl.when(t == pl.num_programs(1) - 1)
    def _():
        st_ref[...] = s_scr[...]
        ko_ref[...] = sk[tm - WINDOW:]
        vo_ref[...] = sv[tm - WINDOW:]


def _const_spec(shape):
    nd = len(shape)
    return pl.BlockSpec(shape, lambda *_: (0,) * nd, pipeline_mode=pl.Buffered(1))


def _rope_tables(pos, d):
    inv = ROPE_THETA ** (-jnp.arange(0, d, 2, dtype=F32) / d)
    ang = pos.astype(F32)[:, None] * inv[None, :]
    c, s = jnp.cos(ang), jnp.sin(ang)
    reps = LANES // d
    return jnp.tile(jnp.concatenate([c, c], axis=1), (1, reps)), jnp.tile(jnp.concatenate([-s, s], axis=1), (1, reps))


def _log_decay():
    return jnp.log(1.0 - 2.0 ** (-5.0 - jnp.arange(RET_HEADS, dtype=F32)))


def _group_matrix():
    g = np.arange(256) // SWA_HD
    return jnp.asarray((g[:, None] == g[None, :]).astype(np.float32) / SWA_HD, dtype=BF16)


def _prompt_layer(x, w, tm):
    B, T, D = x.shape
    nt = T // tm
    lg = _log_decay()
    n = jnp.arange(tm, dtype=F32)
    diff = n[:, None] - n[None, :]
    dmask = jnp.where(diff[None] >= 0, jnp.exp(jnp.maximum(diff, 0.0)[None] * lg[:, None, None]), 0.0)
    cross = jnp.broadcast_to(jnp.exp((n[None, :] + 1.0) * lg[:, None])[:, :, None], (RET_HEADS, tm, RET_DK))
    kdec = jnp.broadcast_to(jnp.exp((tm - 1.0 - n)[None, :] * lg[:, None])[:, :, None], (RET_HEADS, tm, RET_DK))
    decay_chunk = tuple(float(v) for v in np.exp(np.float32(tm) * np.log(
        np.float32(1.0) - np.float32(2.0) ** (-5.0 - np.arange(RET_HEADS, dtype=np.float32)))).astype(np.float32))
    pos = jnp.arange(T, dtype=jnp.int32)
    rcos, rsin = _rope_tables(pos, RET_DK)
    scos, ssin = _rope_tables(pos, SWA_HD)

    tok = lambda b, t: (b, t, 0)
    tab = lambda b, t: (t, 0)
    in_specs = [
        pl.BlockSpec((None, tm, D), tok),
        _const_spec((1, D)),
        _const_spec((D, IN_WIDTH)),
        _const_spec((1, RET_V)),
        _const_spec((1, 256)),
        _const_spec((1, 256)),
        pl.BlockSpec(memory_space=pltpu.SMEM),
        _const_spec((RET_V, D)),
        _const_spec((SWA_Q, D)),
        _const_spec((D, D)),
        pl.BlockSpec((tm, LANES), tab),
        pl.BlockSpec((tm, LANES), tab),
        pl.BlockSpec((tm, LANES), tab),
        pl.BlockSpec((tm, LANES), tab),
        _const_spec((RET_HEADS, tm, tm)),
        _const_spec((RET_HEADS, tm, RET_DK)),
        _const_spec((RET_HEADS, tm, RET_DK)),
        _const_spec((256, 256)),
    ]
    out_shape = (
        jax.ShapeDtypeStruct((B, T, D), F32),
        jax.ShapeDtypeStruct((B, RET_HEADS, RET_DK, RET_DV), F32),
        jax.ShapeDtypeStruct((B, WINDOW, SWA_KV), F32),
        jax.ShapeDtypeStruct((B, WINDOW, SWA_KV), F32),
    )
    out_specs = (
        pl.BlockSpec((None, tm, D), tok),
        pl.BlockSpec((None, RET_HEADS, RET_DK, RET_DV), lambda b, t: (b, 0, 0, 0)),
        pl.BlockSpec((None, WINDOW, SWA_KV), lambda b, t: (b, 0, 0)),
        pl.BlockSpec((None, WINDOW, SWA_KV), lambda b, t: (b, 0, 0)),
    )
    scratch = [
        pltpu.VMEM((RET_HEADS, RET_DK, RET_DV), F32),
        pltpu.VMEM((WINDOW, SWA_KV), F32),
        pltpu.VMEM((WINDOW, SWA_KV), F32),
        pltpu.VMEM((tm, RET_V), BF16),
        pltpu.VMEM((tm, SWA_Q), BF16),
    ]
    return pl.pallas_call(
        functools.partial(_prompt_kernel, decay_chunk=decay_chunk),
        grid=(B, nt),
        in_specs=in_specs,
        out_specs=out_specs,
        out_shape=out_shape,
        scratch_shapes=scratch,
        compiler_params=pltpu.CompilerParams(
            dimension_semantics=("arbitrary", "arbitrary"), vmem_limit_bytes=VMEM_LIMIT),
        name="prompt_layer",
    )(x, w["ng"], w["win"], w["rng"], w["qg"], w["kg"], w["sinks"], w["wbr"], w["wbs"], w["wo"],
      rcos, rsin, scos, ssin, dmask, cross, kdec, _group_matrix())


def _decode_kernel(x_ref, ng_ref, win_ref, rng_ref, qg_ref, kg_ref, sink_ref, wbr_ref, wbs_ref, wo_ref,
                   rcos_ref, rsin_ref, scos_ref, ssin_ref, gmat_ref, rep_ref, seg_ref, segt_ref,
                   st_ref, ck_ref, cv_ref,
                   y_ref, sto_ref, cko_ref, cvo_ref,
                   rq_scr, rk_scr, rv_scr, sq_scr, sk_scr, sv_scr, or_scr, os_scr, *, decay_step):
    i = pl.program_id(0)
    bs = st_ref.shape[0]
    lane = lax.broadcasted_iota(jnp.int32, (1, LANES), 1)
    first_half = (lane % SWA_HD) < (SWA_HD // 2)

    def proj(lo, hi):
        h = _rms_rows(x_ref[...], ng_ref[...]).astype(BF16)
        return _dot(h, win_ref[:, lo:hi])

    @pl.when(i == 0)
    def _():
        rcos = rcos_ref[...]
        rsin = rsin_ref[...]
        rq = proj(C_RQ, C_RK)
        rk = proj(C_RK, C_RV)
        for hd in range(RET_HEADS):
            sl = slice(hd * RET_DK, (hd + 1) * RET_DK)
            rq_scr[:, sl] = _rope128(rq[:, sl], rcos, rsin)
            rk_scr[:, sl] = _rope128(rk[:, sl], rcos, rsin) * (RET_DK ** -0.5)
        rv_scr[...] = proj(C_RV, C_RG)
        gmat = gmat_ref[...]
        scos = scos_ref[...]
        ssin = ssin_ref[...]
        for cb in range(SWA_Q // 256):
            sq_scr[:, cb * 256:(cb + 1) * 256] = _swa_norm_rope(
                proj(C_SQ + cb * 256, C_SQ + (cb + 1) * 256), gmat, qg_ref[...], scos, ssin,
                first_half) * (SWA_HD ** -0.5)
        sk_scr[...] = _swa_norm_rope(proj(C_SK, C_SV), gmat, kg_ref[...], scos, ssin, first_half)
        sv_scr[...] = proj(C_SV, C_SG)

    r0 = pl.multiple_of(i * bs, bs)
    rows = pl.ds(r0, bs)
    row8 = lax.broadcasted_iota(jnp.int32, (bs, 1), 0)
    key_row = lax.broadcasted_iota(jnp.int32, (WINDOW, 1), 0)
    q8 = rq_scr[rows, :]
    k8 = rk_scr[rows, :]
    v8 = rv_scr[rows, :]
    sink_row = sink_ref[...]

    def per_seq(s, o_acc):
        sel = row8 == s
        new_o = []
        for hd in range(RET_HEADS):
            qh = q8[:, hd * RET_DK:(hd + 1) * RET_DK]
            kh = k8[:, hd * RET_DK:(hd + 1) * RET_DK]
            vh = v8[:, hd * RET_DV:(hd + 1) * RET_DV]
            state = st_ref[s, hd]
            inter = _dot(qh.astype(BF16), state.astype(BF16))
            outer = _dot_tn(jnp.where(sel, kh, 0.0).astype(BF16), vh.astype(BF16))
            sto_ref[s, hd] = decay_step[hd] * state + outer
            new_o.append(jnp.where(sel, inter, o_acc[hd]))
        newk = jnp.where(key_row == WINDOW - 1, sk_scr[pl.ds(r0 + s, 1), :], pltpu.roll(ck_ref[s], WINDOW - 1, 0))
        newv = jnp.where(key_row == WINDOW - 1, sv_scr[pl.ds(r0 + s, 1), :], pltpu.roll(cv_ref[s], WINDOW - 1, 0))
        cko_ref[s] = newk
        cvo_ref[s] = newv
        kexp = _dot(newk.astype(BF16), rep_ref[...])
        vexp = _dot(newv.astype(BF16), rep_ref[...])
        prod = kexp * sq_scr[pl.ds(r0 + s, 1), :]
        sc = _dot(prod.astype(BF16), seg_ref[...])
        m = jnp.maximum(jnp.max(sc, axis=0, keepdims=True), sink_row)
        e = jnp.exp(sc - m)
        p = e / (jnp.sum(e, axis=0, keepdims=True) + jnp.exp(sink_row - m))
        pexp = _dot(p.astype(BF16), segt_ref[...])
        os_scr[pl.ds(r0 + s, 1), :] = jnp.sum(pexp * vexp, axis=0, keepdims=True)
        return tuple(new_o)

    zero = jnp.zeros((bs, RET_DV), F32)
    inter = lax.fori_loop(0, bs, per_seq, (zero,) * RET_HEADS)
    for hd in range(RET_HEADS):
        qh = q8[:, hd * RET_DK:(hd + 1) * RET_DK]
        kh = k8[:, hd * RET_DK:(hd + 1) * RET_DK]
        vh = v8[:, hd * RET_DV:(hd + 1) * RET_DV]
        or_scr[rows, hd * RET_DV:(hd + 1) * RET_DV] = (
            jnp.sum(qh * kh, axis=-1, keepdims=True) * vh + decay_step[hd] * inter[hd])

    @pl.when(i == pl.num_programs(0) - 1)
    def _():
        rg = proj(C_RG, C_SQ)
        o_r = or_scr[...]
        parts = []
        for hd in range(RET_HEADS):
            sl = slice(hd * RET_DV, (hd + 1) * RET_DV)
            parts.append(_rms_rows(o_r[:, sl], rng_ref[:, sl]) * (rg[:, sl] * _sigmoid(rg[:, sl])))
        br_r = _dot(jnp.concatenate(parts, axis=1).astype(BF16), wbr_ref[...])
        sg = proj(C_SG, C_MR)
        br_s = _dot((os_scr[...] * (sg * _sigmoid(sg))).astype(BF16), wbs_ref[...])
        merged = _sigmoid(proj(C_MR, C_MS)) * br_r + _sigmoid(proj(C_MS, IN_WIDTH)) * br_s
        y_ref[...] = x_ref[...] + _dot(merged.astype(BF16), wo_ref[...])


def _decode_layer(x, state, cache_k, cache_v, w, past_len, bs):
    nb, D = x.shape
    nsteps = nb // bs
    wc = cache_k.shape[1]
    lg = np.log(np.float32(1.0) - np.float32(2.0) ** (-5.0 - np.arange(RET_HEADS, dtype=np.float32)))
    decay_step = tuple(float(v) for v in np.exp(lg).astype(np.float32))
    pos = jnp.full((1,), past_len, dtype=jnp.int32)
    rcos, rsin = _rope_tables(pos, RET_DK)
    scos, ssin = _rope_tables(pos, SWA_HD)
    lane_kv = (np.arange(SWA_Q) // SWA_HD) // (SWA_HEADS // SWA_KV_HEADS)
    lane_d = np.arange(SWA_Q) % SWA_HD
    rep = np.zeros((SWA_KV, SWA_Q), np.float32)
    rep[lane_kv * SWA_HD + lane_d, np.arange(SWA_Q)] = 1.0
    seg = np.zeros((SWA_Q, LANES), np.float32)
    seg[np.arange(SWA_Q), np.arange(SWA_Q) // SWA_HD] = 1.0
    sink_row = jnp.concatenate([w["sinks"], jnp.full((LANES - SWA_HEADS,), -jnp.inf, F32)])[None, :]

    blk4 = lambda i: (i, 0, 0, 0)
    blk3 = lambda i: (i, 0, 0)
    in_specs = [
        _const_spec((nb, D)),
        _const_spec((1, D)),
        _const_spec((D, IN_WIDTH)),
        _const_spec((1, RET_V)),
        _const_spec((1, 256)),
        _const_spec((1, 256)),
        _const_spec((1, LANES)),
        _const_spec((RET_V, D)),
        _const_spec((SWA_Q, D)),
        _const_spec((D, D)),
        _const_spec((1, LANES)),
        _const_spec((1, LANES)),
        _const_spec((1, LANES)),
        _const_spec((1, LANES)),
        _const_spec((256, 256)),
        _const_spec((SWA_KV, SWA_Q)),
        _const_spec((SWA_Q, LANES)),
        _const_spec((LANES, SWA_Q)),
        pl.BlockSpec((bs, RET_HEADS, RET_DK, RET_DV), blk4),
        pl.BlockSpec((bs, wc, SWA_KV), blk3),
        pl.BlockSpec((bs, wc, SWA_KV), blk3),
    ]
    out_shape = (
        jax.ShapeDtypeStruct((nb, D), F32),
        jax.ShapeDtypeStruct(state.shape, F32),
        jax.ShapeDtypeStruct(cache_k.shape, F32),
        jax.ShapeDtypeStruct(cache_v.shape, F32),
    )
    out_specs = (
        pl.BlockSpec((nb, D), lambda i: (0, 0)),
        pl.BlockSpec((bs, RET_HEADS, RET_DK, RET_DV), blk4),
        pl.BlockSpec((bs, wc, SWA_KV), blk3),
        pl.BlockSpec((bs, wc, SWA_KV), blk3),
    )
    scratch = [
        pltpu.VMEM((nb, RET_QK), F32),
        pltpu.VMEM((nb, RET_QK), F32),
        pltpu.VMEM((nb, RET_V), F32),
        pltpu.VMEM((nb, SWA_Q), F32),
        pltpu.VMEM((nb, SWA_KV), F32),
        pltpu.VMEM((nb, SWA_KV), F32),
        pltpu.VMEM((nb, RET_V), F32),
        pltpu.VMEM((nb, SWA_Q), F32),
    ]
    return pl.pallas_call(
        functools.partial(_decode_kernel, decay_step=decay_step),
        grid=(nsteps,),
        in_specs=in_specs,
        out_specs=out_specs,
        out_shape=out_shape,
        scratch_shapes=scratch,
        compiler_params=pltpu.CompilerParams(dimension_semantics=("arbitrary",), vmem_limit_bytes=VMEM_LIMIT),
        name="decode_layer",
    )(x, w["ng"], w["win"], w["rng"], w["qg"], w["kg"], sink_row, w["wbr"], w["wbs"], w["wo"],
      rcos, rsin, scos, ssin, _group_matrix(), jnp.asarray(rep, BF16), jnp.asarray(seg, BF16),
      jnp.asarray(seg.T, BF16), state, cache_k, cache_v)


def kernel(x_prompt, x_sample, state_ret, cache_swa_k, cache_swa_v, norm_g, w_in, ret_norm_g, swa_q_g, swa_k_g,
           swa_sinks, w_br_ret, w_br_swa, w_out):
    depth = norm_g.shape[0]
    assert depth == 1 and x_sample.shape[1] == 1
    B, T, D = x_prompt.shape
    nb = x_sample.shape[0]
    wc = cache_swa_k.shape[2]
    assert wc == WINDOW and T % PROMPT_TM == 0 and nb % DEC_BS == 0
    l = 0
    w = {
        "ng": norm_g[l][None, :],
        "win": w_in[l].astype(BF16),
        "rng": ret_norm_g[l].reshape(1, RET_V),
        "qg": jnp.tile(swa_q_g[l], 256 // SWA_HD)[None, :],
        "kg": jnp.tile(swa_k_g[l], 256 // SWA_HD)[None, :],
        "sinks": swa_sinks[l],
        "wbr": w_br_ret[l].astype(BF16),
        "wbs": w_br_swa[l].astype(BF16),
        "wo": w_out[l].astype(BF16),
    }
    yp, rp, kp, vp = _prompt_layer(x_prompt, w, PROMPT_TM)
    ys, rs, ks, vs = _decode_layer(x_sample[:, 0, :], state_ret[l], cache_swa_k[l].reshape(nb, wc, SWA_KV),
                                   cache_swa_v[l].reshape(nb, wc, SWA_KV), w, PAST_LEN, DEC_BS)
    return (yp, ys[:, None, :], rp[None], rs[None],
            kp.reshape(1, B, WINDOW, SWA_KV_HEADS, SWA_HD), vp.reshape(1, B, WINDOW, SWA_KV_HEADS, SWA_HD),
            ks.reshape(1, nb, wc, SWA_KV_HEADS, SWA_HD), vs.reshape(1, nb, wc, SWA_KV_HEADS, SWA_HD))
```

```python
import functools
import math

import numpy as np
import jax
import jax.numpy as jnp
from jax import lax
from jax.experimental import pallas as pl
from jax.experimental.pallas import tpu as pltpu

F32 = jnp.float32
BF16 = jnp.bfloat16

D_MODEL = 1024
RET_HEADS = 4
RET_DK = 128
RET_DV = 256
RET_QK = RET_HEADS * RET_DK
RET_V = RET_HEADS * RET_DV
SWA_HEADS = 16
SWA_KV_HEADS = 4
SWA_HD = 64
SWA_Q = SWA_HEADS * SWA_HD
SWA_KV = SWA_KV_HEADS * SWA_HD
WINDOW = 128
ROPE_THETA = 10000.0
EPS = 1e-6
PAST_LEN = 8192

C_RQ = 0
C_RK = C_RQ + RET_QK
C_RV = C_RK + RET_QK
C_RG = C_RV + RET_V
C_SQ = C_RG + RET_V
C_SK = C_SQ + SWA_Q
C_SV = C_SK + SWA_KV
C_SG = C_SV + SWA_KV
C_MR = C_SG + SWA_Q
C_MS = C_MR + D_MODEL
IN_WIDTH = C_MS + D_MODEL

LANES = 128
PROMPT_TM = 256
DEC_BS = 8
VMEM_LIMIT = 56 * 1024 * 1024


def _dot(a, b):
    return jnp.dot(a, b, preferred_element_type=F32)


def _dot_nt(a, b):
    return lax.dot_general(a, b, (((1,), (1,)), ((), ())), preferred_element_type=F32)


def _dot_tn(a, b):
    return lax.dot_general(a, b, (((0,), (0,)), ((), ())), preferred_element_type=F32)


def _sigmoid(x):
    return 1.0 / (1.0 + jnp.exp(-x))


def _rms_rows(x, g):
    return x * lax.rsqrt(jnp.mean(x * x, axis=-1, keepdims=True) + EPS) * g


def _group_mean_sq(x, gmat):
    x2 = x * x
    hi = x2.astype(BF16)
    lo = (x2 - hi.astype(F32)).astype(BF16)
    return _dot(hi, gmat) + _dot(lo, gmat)


def _rope128(x, cos, sin_signed):
    return x * cos + pltpu.roll(x, 64, 1) * sin_signed


def _rope64(x, cos, sin_signed, first_half):
    rot = jnp.where(first_half, pltpu.roll(x, 96, 1), pltpu.roll(x, 32, 1))
    return x * cos + rot * sin_signed


def _swa_norm_rope(x, gmat, gain, cos, sin_signed, first_half):
    xn = x * lax.rsqrt(_group_mean_sq(x, gmat) + EPS) * gain
    cols = [_rope64(xn[:, c * LANES:(c + 1) * LANES], cos, sin_signed, first_half) for c in range(2)]
    return jnp.concatenate(cols, axis=1)


def _prompt_kernel(x_ref, ng_ref, win_ref, rng_ref, qg_ref, kg_ref, sink_ref, wbr_ref, wbs_ref, wo_ref,
                   rcos_ref, rsin_ref, scos_ref, ssin_ref, dmask_ref, cross_ref, kdec_ref, gmat_ref,
                   y_ref, st_ref, ko_ref, vo_ref,
                   s_scr, pk_scr, pv_scr, or_scr, os_scr, *, decay_chunk):
    t = pl.program_id(1)
    tm = x_ref.shape[0]
    nblk = tm // WINDOW

    @pl.when(t == 0)
    def _():
        s_scr[...] = jnp.zeros_like(s_scr)
        pk_scr[...] = jnp.zeros_like(pk_scr)
        pv_scr[...] = jnp.zeros_like(pv_scr)

    x = x_ref[...]
    h = _rms_rows(x, ng_ref[...]).astype(BF16)

    def proj(lo, hi):
        return _dot(h, win_ref[:, lo:hi])

    gmat = gmat_ref[...]
    scos = scos_ref[...]
    ssin = ssin_ref[...]
    lane = lax.broadcasted_iota(jnp.int32, (1, LANES), 1)
    first_half = (lane % SWA_HD) < (SWA_HD // 2)
    low = lane < SWA_HD

    rq = proj(C_RQ, C_RK)
    rk = proj(C_RK, C_RV)
    rv = proj(C_RV, C_RG).astype(BF16)
    rcos = rcos_ref[...]
    rsin = rsin_ref[...]
    k_scale = RET_DK ** -0.5
    scores, inter, vs = [], [], []
    for hd in range(RET_HEADS):
        q = _rope128(rq[:, hd * RET_DK:(hd + 1) * RET_DK], rcos, rsin)
        k = _rope128(rk[:, hd * RET_DK:(hd + 1) * RET_DK], rcos, rsin) * k_scale
        v = rv[:, hd * RET_DV:(hd + 1) * RET_DV]
        state = s_scr[hd]
        scores.append(_dot_nt(q.astype(BF16), k.astype(BF16)))
        inter.append(_dot((q * cross_ref[hd]).astype(BF16), state.astype(BF16)))
        s_scr[hd] = decay_chunk[hd] * state + _dot_tn((k * kdec_ref[hd]).astype(BF16), v)
        vs.append(v)
    sq_cols = []
    sk = sv = None
    for hd in range(RET_HEADS):
        g = proj(C_RG + hd * RET_DV, C_RG + (hd + 1) * RET_DV)
        sq_cols.append(_swa_norm_rope(proj(C_SQ + hd * 256, C_SQ + (hd + 1) * 256), gmat, qg_ref[...], scos, ssin,
                                      first_half) * (SWA_HD ** -0.5))
        if hd == 0:
            sk = _swa_norm_rope(proj(C_SK, C_SV), gmat, kg_ref[...], scos, ssin, first_half)
        if hd == 1:
            sv = proj(C_SV, C_SG)
        o = _dot((scores[hd] * dmask_ref[hd]).astype(BF16), vs[hd]) + inter[hd]
        on = _rms_rows(o, rng_ref[:, hd * RET_DV:(hd + 1) * RET_DV])
        or_scr[:, hd * RET_DV:(hd + 1) * RET_DV] = (on * (g * _sigmoid(g))).astype(BF16)

    row_i = lax.broadcasted_iota(jnp.int32, (WINDOW, 2 * WINDOW), 0)
    col_j = lax.broadcasted_iota(jnp.int32, (WINDOW, 2 * WINDOW), 1)
    allowed, kbds, vbds = [], {}, {}
    for c in range(nblk):
        lower = row_i + 1
        if c == 0:
            lower = jnp.maximum(lower, jnp.where(t == 0, WINDOW, 0))
        allowed.append((col_j >= lower) & (col_j <= row_i + WINDOW))

    def block_diag(c, j, src, prev_scr):
        r0 = c * WINDOW
        prev = prev_scr[...] if c == 0 else src[r0 - WINDOW:r0]
        cat = jnp.concatenate([prev, src[r0:r0 + WINDOW]], axis=0)
        col, half = divmod(j, 2)
        a = cat[:, col * LANES:(col + 1) * LANES]
        ar = pltpu.roll(a, SWA_HD, 1)
        if half == 0:
            bd = jnp.concatenate([jnp.where(low, a, 0.0), jnp.where(low, 0.0, ar)], axis=0)
        else:
            bd = jnp.concatenate([jnp.where(low, ar, 0.0), jnp.where(low, 0.0, a)], axis=0)
        return bd.astype(BF16)

    pairs = [(c, p) for c in range(nblk) for p in range(SWA_HEADS // 2)]

    def qk(i):
        c, p = pairs[i]
        j = p // 2
        if (c, j) not in kbds:
            kbds[(c, j)] = block_diag(c, j, sk, pk_scr)
            vbds[(c, j)] = block_diag(c, j, sv, pv_scr)
        qp = sq_cols[p // 2][c * WINDOW:(c + 1) * WINDOW, (p % 2) * LANES:(p % 2 + 1) * LANES].astype(BF16)
        return _dot_nt(qp, kbds[(c, j)])

    chunks = {}

    def filler(i):
        kind, cb = divmod(i, 4)
        if kind == 0:
            chunks["sg", cb] = proj(C_SG + cb * 256, C_SG + (cb + 1) * 256)
        elif kind == 1:
            chunks["mr", cb] = _sigmoid(proj(C_MR + cb * 256, C_MR + (cb + 1) * 256))
        elif kind == 2:
            chunks["ms", cb] = _sigmoid(proj(C_MS + cb * 256, C_MS + (cb + 1) * 256))
        else:
            chunks["br", cb] = _dot(or_scr[...], wbr_ref[:, cb * 256:(cb + 1) * 256])

    assert nblk == 2
    pairs.sort(key=lambda cp: (cp[1], cp[0]))
    order = [0, 4, 8, 12, 1, 5, 9, 13, 2, 6, 10, 14, 3, 7, 11, 15]
    s2_next = qk(0)
    for i, (c, p) in enumerate(pairs):
        s2 = s2_next
        if i + 1 < len(pairs):
            s2_next = qk(i + 1)
        filler(order[i])
        es, invs = [], []
        for a in range(2):
            sink = sink_ref[2 * p + a]
            s = jnp.where(allowed[c], s2[:, a * 2 * WINDOW:(a + 1) * 2 * WINDOW], -jnp.inf)
            m = jnp.maximum(jnp.max(s, axis=-1, keepdims=True), sink)
            e = jnp.exp(s - m)
            invs.append(1.0 / (jnp.sum(e, axis=-1, keepdims=True) + jnp.exp(sink - m)))
            es.append(e.astype(BF16))
        o2 = _dot(jnp.concatenate(es, axis=1), vbds[(c, p // 2)])
        o2 = o2 * jnp.where(low, invs[0], invs[1])
        g = chunks["sg", p // 2][c * WINDOW:(c + 1) * WINDOW, (p % 2) * LANES:(p % 2 + 1) * LANES]
        os_scr[c * WINDOW:(c + 1) * WINDOW, p * LANES:(p + 1) * LANES] = (o2 * (g * _sigmoid(g))).astype(BF16)
    pk_scr[...] = sk[tm - WINDOW:]
    pv_scr[...] = sv[tm - WINDOW:]

    merged = []
    for cb in range(D_MODEL // 256):
        br_s = _dot(os_scr[...], wbs_ref[:, cb * 256:(cb + 1) * 256])
        merged.append((chunks["mr", cb] * chunks["br", cb] + chunks["ms", cb] * br_s).astype(BF16))
    y_ref[...] = x + _dot(jnp.concatenate(merged, axis=1), wo_ref[...])

    @pl.when(t == pl.num_programs(1) - 1)
    def _():
        st_ref[...] = s_scr[...]
        ko_ref[...] = sk[tm - WINDOW:]
        vo_ref[...] = sv[tm - WINDOW:]


def _const_spec(shape):
    nd = len(shape)
    return pl.BlockSpec(shape, lambda *_: (0,) * nd, pipeline_mode=pl.Buffered(1))


def _rope_tables(pos, d):
    inv = ROPE_THETA ** (-jnp.arange(0, d, 2, dtype=F32) / d)
    ang = pos.astype(F32)[:, None] * inv[None, :]
    c, s = jnp.cos(ang), jnp.sin(ang)
    reps = LANES // d
    return jnp.tile(jnp.concatenate([c, c], axis=1), (1, reps)), jnp.tile(jnp.concatenate([-s, s], axis=1), (1, reps))


def _log_decay():
    return jnp.log(1.0 - 2.0 ** (-5.0 - jnp.arange(RET_HEADS, dtype=F32)))


def _group_matrix():
    g = np.arange(256) // SWA_HD
    return jnp.asarray((g[:, None] == g[None, :]).astype(np.float32) / SWA_HD, dtype=BF16)


def _prompt_layer(x, w, tm):
    B, T, D = x.shape
    nt = T // tm
    lg = _log_decay()
    n = jnp.arange(tm, dtype=F32)
    diff = n[:, None] - n[None, :]
    dmask = jnp.where(diff[None] >= 0, jnp.exp(jnp.maximum(diff, 0.0)[None] * lg[:, None, None]), 0.0)
    cross = jnp.broadcast_to(jnp.exp((n[None, :] + 1.0) * lg[:, None])[:, :, None], (RET_HEADS, tm, RET_DK))
    kdec = jnp.broadcast_to(jnp.exp((tm - 1.0 - n)[None, :] * lg[:, None])[:, :, None], (RET_HEADS, tm, RET_DK))
    decay_chunk = tuple(float(v) for v in np.exp(np.float32(tm) * np.log(
        np.float32(1.0) - np.float32(2.0) ** (-5.0 - np.arange(RET_HEADS, dtype=np.float32)))).astype(np.float32))
    pos = jnp.arange(T, dtype=jnp.int32)
    rcos, rsin = _rope_tables(pos, RET_DK)
    scos, ssin = _rope_tables(pos, SWA_HD)

    tok = lambda b, t: (b, t, 0)
    tab = lambda b, t: (t, 0)
    in_specs = [
        pl.BlockSpec((None, tm, D), tok),
        _const_spec((1, D)),
        _const_spec((D, IN_WIDTH)),
        _const_spec((1, RET_V)),
        _const_spec((1, 256)),
        _const_spec((1, 256)),
        pl.BlockSpec(memory_space=pltpu.SMEM),
        _const_spec((RET_V, D)),
        _const_spec((SWA_Q, D)),
        _const_spec((D, D)),
        pl.BlockSpec((tm, LANES), tab),
        pl.BlockSpec((tm, LANES), tab),
        pl.BlockSpec((tm, LANES), tab),
        pl.BlockSpec((tm, LANES), tab),
        _const_spec((RET_HEADS, tm, tm)),
        _const_spec((RET_HEADS, tm, RET_DK)),
        _const_spec((RET_HEADS, tm, RET_DK)),
        _const_spec((256, 256)),
    ]
    out_shape = (
        jax.ShapeDtypeStruct((B, T, D), F32),
        jax.ShapeDtypeStruct((B, RET_HEADS, RET_DK, RET_DV), F32),
        jax.ShapeDtypeStruct((B, WINDOW, SWA_KV), F32),
        jax.ShapeDtypeStruct((B, WINDOW, SWA_KV), F32),
    )
    out_specs = (
        pl.BlockSpec((None, tm, D), tok),
        pl.BlockSpec((None, RET_HEADS, RET_DK, RET_DV), lambda b, t: (b, 0, 0, 0)),
        pl.BlockSpec((None, WINDOW, SWA_KV), lambda b, t: (b, 0, 0)),
        pl.BlockSpec((None, WINDOW, SWA_KV), lambda b, t: (b, 0, 0)),
    )
    scratch = [
        pltpu.VMEM((RET_HEADS, RET_DK, RET_DV), F32),
        pltpu.VMEM((WINDOW, SWA_KV), F32),
        pltpu.VMEM((WINDOW, SWA_KV), F32),
        pltpu.VMEM((tm, RET_V), BF16),
        pltpu.VMEM((tm, SWA_Q), BF16),
    ]
    return pl.pallas_call(
        functools.partial(_prompt_kernel, decay_chunk=decay_chunk),
        grid=(B, nt),
        in_specs=in_specs,
        out_specs=out_specs,
        out_shape=out_shape,
        scratch_shapes=scratch,
        compiler_params=pltpu.CompilerParams(
            dimension_semantics=("arbitrary", "arbitrary"), vmem_limit_bytes=VMEM_LIMIT),
        name="prompt_layer",
    )(x, w["ng"], w["win"], w["rng"], w["qg"], w["kg"], w["sinks"], w["wbr"], w["wbs"], w["wo"],
      rcos, rsin, scos, ssin, dmask, cross, kdec, _group_matrix())


def _decode_kernel(x_ref, ng_ref, win_ref, rng_ref, qg_ref, kg_ref, sink_ref, wbr_ref, wbs_ref, wo_ref,
                   rcos_ref, rsin_ref, scos_ref, ssin_ref, gmat_ref, rep_ref, seg_ref, segt_ref,
                   st_ref, ck_ref, cv_ref,
                   y_ref, sto_ref, cko_ref, cvo_ref,
                   rq_scr, rk_scr, rv_scr, sq_scr, sk_scr, sv_scr, or_scr, os_scr, *, decay_step):
    i = pl.program_id(0)
    bs = st_ref.shape[0]
    lane = lax.broadcasted_iota(jnp.int32, (1, LANES), 1)
    first_half = (lane % SWA_HD) < (SWA_HD // 2)

    def proj(lo, hi):
        h = _rms_rows(x_ref[...], ng_ref[...]).astype(BF16)
        return _dot(h, win_ref[:, lo:hi])

    @pl.when(i == 0)
    def _():
        rcos = rcos_ref[...]
        rsin = rsin_ref[...]
        rq = proj(C_RQ, C_RK)
        rk = proj(C_RK, C_RV)
        for hd in range(RET_HEADS):
            sl = slice(hd * RET_DK, (hd + 1) * RET_DK)
            rq_scr[:, sl] = _rope128(rq[:, sl], rcos, rsin)
            rk_scr[:, sl] = _rope128(rk[:, sl], rcos, rsin) * (RET_DK ** -0.5)
        rv_scr[...] = proj(C_RV, C_RG)
        gmat = gmat_ref[...]
        scos = scos_ref[...]
        ssin = ssin_ref[...]
        for cb in range(SWA_Q // 256):
            sq_scr[:, cb * 256:(cb + 1) * 256] = _swa_norm_rope(
                proj(C_SQ + cb * 256, C_SQ + (cb + 1) * 256), gmat, qg_ref[...], scos, ssin,
                first_half) * (SWA_HD ** -0.5)
        sk_scr[...] = _swa_norm_rope(proj(C_SK, C_SV), gmat, kg_ref[...], scos, ssin, first_half)
        sv_scr[...] = proj(C_SV, C_SG)

    r0 = pl.multiple_of(i * bs, bs)
    rows = pl.ds(r0, bs)
    row8 = lax.broadcasted_iota(jnp.int32, (bs, 1), 0)
    key_row = lax.broadcasted_iota(jnp.int32, (WINDOW, 1), 0)
    q8 = rq_scr[rows, :]
    k8 = rk_scr[rows, :]
    v8 = rv_scr[rows, :]
    sink_row = sink_ref[...]

    def per_seq(s, o_acc):
        sel = row8 == s
        new_o = []
        for hd in range(RET_HEADS):
            qh = q8[:, hd * RET_DK:(hd + 1) * RET_DK]
            kh = k8[:, hd * RET_DK:(hd + 1) * RET_DK]
            vh = v8[:, hd * RET_DV:(hd + 1) * RET_DV]
            state = st_ref[s, hd]
            inter = _dot(qh.astype(BF16), state.astype(BF16))
            outer = _dot_tn(jnp.where(sel, kh, 0.0).astype(BF16), vh.astype(BF16))
            sto_ref[s, hd] = decay_step[hd] * state + outer
            new_o.append(jnp.where(sel, inter, o_acc[hd]))
        newk = jnp.where(key_row == WINDOW - 1, sk_scr[pl.ds(r0 + s, 1), :], pltpu.roll(ck_ref[s], WINDOW - 1, 0))
        newv = jnp.where(key_row == WINDOW - 1, sv_scr[pl.ds(r0 + s, 1), :], pltpu.roll(cv_ref[s], WINDOW - 1, 0))
        cko_ref[s] = newk
        cvo_ref[s] = newv
        kexp = _dot(newk.astype(BF16), rep_ref[...])
        vexp = _dot(newv.astype(BF16), rep_ref[...])
        prod = kexp * sq_scr[pl.ds(r0 + s, 1), :]
        sc = _dot(prod.astype(BF16), seg_ref[...])
        m = jnp.maximum(jnp.max(sc, axis=0, keepdims=True), sink_row)
        e = jnp.exp(sc - m)
        p = e / (jnp.sum(e, axis=0, keepdims=True) + jnp.exp(sink_row - m))
        pexp = _dot(p.astype(BF16), segt_ref[...])
        os_scr[pl.ds(r0 + s, 1), :] = jnp.sum(pexp * vexp, axis=0, keepdims=True)
        return tuple(new_o)

    zero = jnp.zeros((bs, RET_DV), F32)
    inter = lax.fori_loop(0, bs, per_seq, (zero,) * RET_HEADS)
    for hd in range(RET_HEADS):
        qh = q8[:, hd * RET_DK:(hd + 1) * RET_DK]
        kh = k8[:, hd * RET_DK:(hd + 1) * RET_DK]
        vh = v8[:, hd * RET_DV:(hd + 1) * RET_DV]
        or_scr[rows, hd * RET_DV:(hd + 1) * RET_DV] = (
            jnp.sum(qh * kh, axis=-1, keepdims=True) * vh + decay_step[hd] * inter[hd])

    @pl.when(i == pl.num_programs(0) - 1)
    def _():
        rg = proj(C_RG, C_SQ)
        o_r = or_scr[...]
        parts = []
        for hd in range(RET_HEADS):
            sl = slice(hd * RET_DV, (hd + 1) * RET_DV)
            parts.append(_rms_rows(o_r[:, sl], rng_ref[:, sl]) * (rg[:, sl] * _sigmoid(rg[:, sl])))
        br_r = _dot(jnp.concatenate(parts, axis=1).astype(BF16), wbr_ref[...])
        sg = proj(C_SG, C_MR)
        br_s = _dot((os_scr[...] * (sg * _sigmoid(sg))).astype(BF16), wbs_ref[...])
        merged = _sigmoid(proj(C_MR, C_MS)) * br_r + _sigmoid(proj(C_MS, IN_WIDTH)) * br_s
        y_ref[...] = x_ref[...] + _dot(merged.astype(BF16), wo_ref[...])


def _decode_layer(x, state, cache_k, cache_v, w, past_len, bs):
    nb, D = x.shape
    nsteps = nb // bs
    wc = cache_k.shape[1]
    lg = np.log(np.float32(1.0) - np.float32(2.0) ** (-5.0 - np.arange(RET_HEADS, dtype=np.float32)))
    decay_step = tuple(float(v) for v in np.exp(lg).astype(np.float32))
    pos = jnp.full((1,), past_len, dtype=jnp.int32)
    rcos, rsin = _rope_tables(pos, RET_DK)
    scos, ssin = _rope_tables(pos, SWA_HD)
    lane_kv = (np.arange(SWA_Q) // SWA_HD) // (SWA_HEADS // SWA_KV_HEADS)
    lane_d = np.arange(SWA_Q) % SWA_HD
    rep = np.zeros((SWA_KV, SWA_Q), np.float32)
    rep[lane_kv * SWA_HD + lane_d, np.arange(SWA_Q)] = 1.0
    seg = np.zeros((SWA_Q, LANES), np.float32)
    seg[np.arange(SWA_Q), np.arange(SWA_Q) // SWA_HD] = 1.0
    sink_row = jnp.concatenate([w["sinks"], jnp.full((LANES - SWA_HEADS,), -jnp.inf, F32)])[None, :]

    blk4 = lambda i: (i, 0, 0, 0)
    blk3 = lambda i: (i, 0, 0)
    in_specs = [
        _const_spec((nb, D)),
        _const_spec((1, D)),
        _const_spec((D, IN_WIDTH)),
        _const_spec((1, RET_V)),
        _const_spec((1, 256)),
        _const_spec((1, 256)),
        _const_spec((1, LANES)),
        _const_spec((RET_V, D)),
        _const_spec((SWA_Q, D)),
        _const_spec((D, D)),
        _const_spec((1, LANES)),
        _const_spec((1, LANES)),
        _const_spec((1, LANES)),
        _const_spec((1, LANES)),
        _const_spec((256, 256)),
        _const_spec((SWA_KV, SWA_Q)),
        _const_spec((SWA_Q, LANES)),
        _const_spec((LANES, SWA_Q)),
        pl.BlockSpec((bs, RET_HEADS, RET_DK, RET_DV), blk4),
        pl.BlockSpec((bs, wc, SWA_KV), blk3),
        pl.BlockSpec((bs, wc, SWA_KV), blk3),
    ]
    out_shape = (
        jax.ShapeDtypeStruct((nb, D), F32),
        jax.ShapeDtypeStruct(state.shape, F32),
        jax.ShapeDtypeStruct(cache_k.shape, F32),
        jax.ShapeDtypeStruct(cache_v.shape, F32),
    )
    out_specs = (
        pl.BlockSpec((nb, D), lambda i: (0, 0)),
        pl.BlockSpec((bs, RET_HEADS, RET_DK, RET_DV), blk4),
        pl.BlockSpec((bs, wc, SWA_KV), blk3),
        pl.BlockSpec((bs, wc, SWA_KV), blk3),
    )
    scratch = [
        pltpu.VMEM((nb, RET_QK), F32),
        pltpu.VMEM((nb, RET_QK), F32),
        pltpu.VMEM((nb, RET_V), F32),
        pltpu.VMEM((nb, SWA_Q), F32),
        pltpu.VMEM((nb, SWA_KV), F32),
        pltpu.VMEM((nb, SWA_KV), F32),
        pltpu.VMEM((nb, RET_V), F32),
        pltpu.VMEM((nb, SWA_Q), F32),
    ]
    return pl.pallas_call(
        functools.partial(_decode_kernel, decay_step=decay_step),
        grid=(nsteps,),
        in_specs=in_specs,
        out_specs=out_specs,
        out_shape=out_shape,
        scratch_shapes=scratch,
        compiler_params=pltpu.CompilerParams(dimension_semantics=("arbitrary",), vmem_limit_bytes=VMEM_LIMIT),
        name="decode_layer",
    )(x, w["ng"], w["win"], w["rng"], w["qg"], w["kg"], sink_row, w["wbr"], w["wbs"], w["wo"],
      rcos, rsin, scos, ssin, _group_matrix(), jnp.asarray(rep, BF16), jnp.asarray(seg, BF16),
      jnp.asarray(seg.T, BF16), state, cache_k, cache_v)


def kernel(x_prompt, x_sample, state_ret, cache_swa_k, cache_swa_v, norm_g, w_in, ret_norm_g, swa_q_g, swa_k_g,
           swa_sinks, w_br_ret, w_br_swa, w_out):
    depth = norm_g.shape[0]
    assert depth == 1 and x_sample.shape[1] == 1
    B, T, D = x_prompt.shape
    nb = x_sample.shape[0]
    wc = cache_swa_k.shape[2]
    assert wc == WINDOW and T % PROMPT_TM == 0 and nb % DEC_BS == 0
    l = 0
    w = {
        "ng": norm_g[l][None, :],
        "win": w_in[l].astype(BF16),
        "rng": ret_norm_g[l].reshape(1, RET_V),
        "qg": jnp.tile(swa_q_g[l], 256 // SWA_HD)[None, :],
        "kg": jnp.tile(swa_k_g[l], 256 // SWA_HD)[None, :],
        "sinks": swa_sinks[l],
        "wbr": w_br_ret[l].astype(BF16),
        "wbs": w_br_swa[l].astype(BF16),
        "wo": w_out[l].astype(BF16),
    }
    yp, rp, kp, vp = _prompt_layer(x_prompt, w, PROMPT_TM)
    ys, rs, ks, vs = _decode_layer(x_sample[:, 0, :], state_ret[l], cache_swa_k[l].reshape(nb, wc, SWA_KV),
                                   cache_swa_v[l].reshape(nb, wc, SWA_KV), w, PAST_LEN, DEC_BS)
    return (yp, ys[:, None, :], rp[None], rs[None],
            kp.reshape(1, B, WINDOW, SWA_KV_HEADS, SWA_HD), vp.reshape(1, B, WINDOW, SWA_KV_HEADS, SWA_HD),
            ks.reshape(1, nb, wc, SWA_KV_HEADS, SWA_HD), vs.reshape(1, nb, wc, SWA_KV_HEADS, SWA_HD))
```

```python
import functools
import math

import numpy as np
import jax
import jax.numpy as jnp
from jax import lax
from jax.experimental import pallas as pl
from jax.experimental.pallas import tpu as pltpu

F32 = jnp.float32
BF16 = jnp.bfloat16

D_MODEL = 1024
RET_HEADS = 4
RET_DK = 128
RET_DV = 256
RET_QK = RET_HEADS * RET_DK
RET_V = RET_HEADS * RET_DV
SWA_HEADS = 16
SWA_KV_HEADS = 4
SWA_HD = 64
SWA_Q = SWA_HEADS * SWA_HD
SWA_KV = SWA_KV_HEADS * SWA_HD
WINDOW = 128
ROPE_THETA = 10000.0
EPS = 1e-6
PAST_LEN = 8192

C_RQ = 0
C_RK = C_RQ + RET_QK
C_RV = C_RK + RET_QK
C_RG = C_RV + RET_V
C_SQ = C_RG + RET_V
C_SK = C_SQ + SWA_Q
C_SV = C_SK + SWA_KV
C_SG = C_SV + SWA_KV
C_MR = C_SG + SWA_Q
C_MS = C_MR + D_MODEL
IN_WIDTH = C_MS + D_MODEL

LANES = 128
PROMPT_TM = 256
DEC_BS = 8
VMEM_LIMIT = 56 * 1024 * 1024


def _dot(a, b):
    return jnp.dot(a, b, preferred_element_type=F32)


def _dot_nt(a, b):
    return lax.dot_general(a, b, (((1,), (1,)), ((), ())), preferred_element_type=F32)


def _dot_tn(a, b):
    return lax.dot_general(a, b, (((0,), (0,)), ((), ())), preferred_element_type=F32)


def _sigmoid(x):
    return 1.0 / (1.0 + jnp.exp(-x))


def _rms_rows(x, g):
    return x * lax.rsqrt(jnp.mean(x * x, axis=-1, keepdims=True) + EPS) * g


def _group_mean_sq(x, gmat):
    x2 = x * x
    hi = x2.astype(BF16)
    lo = (x2 - hi.astype(F32)).astype(BF16)
    return _dot(hi, gmat) + _dot(lo, gmat)


def _rope128(x, cos, sin_signed):
    return x * cos + pltpu.roll(x, 64, 1) * sin_signed


def _rope64(x, cos, sin_signed, first_half):
    rot = jnp.where(first_half, pltpu.roll(x, 96, 1), pltpu.roll(x, 32, 1))
    return x * cos + rot * sin_signed


def _swa_norm_rope(x, gmat, gain, cos, sin_signed, first_half):
    xn = x * lax.rsqrt(_group_mean_sq(x, gmat) + EPS) * gain
    cols = [_rope64(xn[:, c * LANES:(c + 1) * LANES], cos, sin_signed, first_half) for c in range(2)]
    return jnp.concatenate(cols, axis=1)


def _prompt_kernel(x_ref, ng_ref, win_ref, rng_ref, qg_ref, kg_ref, sink_ref, wbr_ref, wbs_ref, wo_ref,
                   rcos_ref, rsin_ref, scos_ref, ssin_ref, dmask_ref, cross_ref, kdec_ref, gmat_ref,
                   y_ref, st_ref, ko_ref, vo_ref,
                   s_scr, pk_scr, pv_scr, or_scr, os_scr, *, decay_chunk):
    t = pl.program_id(1)
    tm = x_ref.shape[0]
    nblk = tm // WINDOW

    @pl.when(t == 0)
    def _():
        s_scr[...] = jnp.zeros_like(s_scr)
        pk_scr[...] = jnp.zeros_like(pk_scr)
        pv_scr[...] = jnp.zeros_like(pv_scr)

    x = x_ref[...]
    h = _rms_rows(x, ng_ref[...]).astype(BF16)

    def proj(lo, hi):
        return _dot(h, win_ref[:, lo:hi])

    gmat = gmat_ref[...]
    scos = scos_ref[...]
    ssin = ssin_ref[...]
    lane = lax.broadcasted_iota(jnp.int32, (1, LANES), 1)
    first_half = (lane % SWA_HD) < (SWA_HD // 2)
    low = lane < SWA_HD

    rq = proj(C_RQ, C_RK)
    rk = proj(C_RK, C_RV)
    rv = proj(C_RV, C_RG).astype(BF16)
    rcos = rcos_ref[...]
    rsin = rsin_ref[...]
    k_scale = RET_DK ** -0.5
    scores, inter, vs = [], [], []
    for hd in range(RET_HEADS):
        q = _rope128(rq[:, hd * RET_DK:(hd + 1) * RET_DK], rcos, rsin)
        k = _rope128(rk[:, hd * RET_DK:(hd + 1) * RET_DK], rcos, rsin) * k_scale
        v = rv[:, hd * RET_DV:(hd + 1) * RET_DV]
        state = s_scr[hd]
        scores.append(_dot_nt(q.astype(BF16), k.astype(BF16)))
        inter.append(_dot((q * cross_ref[hd]).astype(BF16), state.astype(BF16)))
        s_scr[hd] = decay_chunk[hd] * state + _dot_tn((k * kdec_ref[hd]).astype(BF16), v)
        vs.append(v)
    sq_cols = []
    sk = sv = None
    for hd in range(RET_HEADS):
        g = proj(C_RG + hd * RET_DV, C_RG + (hd + 1) * RET_DV)
        sq_cols.append(_swa_norm_rope(proj(C_SQ + hd * 256, C_SQ + (hd + 1) * 256), gmat, qg_ref[...], scos, ssin,
                                      first_half) * (SWA_HD ** -0.5))
        if hd == 0:
            sk = _swa_norm_rope(proj(C_SK, C_SV), gmat, kg_ref[...], scos, ssin, first_half)
        if hd == 1:
            sv = proj(C_SV, C_SG)
        o = _dot((scores[hd] * dmask_ref[hd]).astype(BF16), vs[hd]) + inter[hd]
        on = _rms_rows(o, rng_ref[:, hd * RET_DV:(hd + 1) * RET_DV])
        or_scr[:, hd * RET_DV:(hd + 1) * RET_DV] = (on * (g * _sigmoid(g))).astype(BF16)

    row_i = lax.broadcasted_iota(jnp.int32, (WINDOW, 2 * WINDOW), 0)
    col_j = lax.broadcasted_iota(jnp.int32, (WINDOW, 2 * WINDOW), 1)
    allowed, kbds, vbds = [], {}, {}
    for c in range(nblk):
        lower = row_i + 1
        if c == 0:
            lower = jnp.maximum(lower, jnp.where(t == 0, WINDOW, 0))
        allowed.append((col_j >= lower) & (col_j <= row_i + WINDOW))

    def block_diag(c, j, src, prev_scr):
        r0 = c * WINDOW
        prev = prev_scr[...] if c == 0 else src[r0 - WINDOW:r0]
        cat = jnp.concatenate([prev, src[r0:r0 + WINDOW]], axis=0)
        col, half = divmod(j, 2)
        a = cat[:, col * LANES:(col + 1) * LANES]
        ar = pltpu.roll(a, SWA_HD, 1)
        if half == 0:
            bd = jnp.concatenate([jnp.where(low, a, 0.0), jnp.where(low, 0.0, ar)], axis=0)
        else:
            bd = jnp.concatenate([jnp.where(low, ar, 0.0), jnp.where(low, 0.0, a)], axis=0)
        return bd.astype(BF16)

    pairs = [(c, p) for c in range(nblk) for p in range(SWA_HEADS // 2)]

    def qk(i):
        c, p = pairs[i]
        j = p // 2
        if (c, j) not in kbds:
            kbds[(c, j)] = block_diag(c, j, sk, pk_scr)
            vbds[(c, j)] = block_diag(c, j, sv, pv_scr)
        qp = sq_cols[p // 2][c * WINDOW:(c + 1) * WINDOW, (p % 2) * LANES:(p % 2 + 1) * LANES].astype(BF16)
        return _dot_nt(qp, kbds[(c, j)])

    chunks = {}

    def filler(i):
        kind, cb = divmod(i, 4)
        if kind == 0:
            chunks["sg", cb] = proj(C_SG + cb * 256, C_SG + (cb + 1) * 256)
        elif kind == 1:
            chunks["mr", cb] = _sigmoid(proj(C_MR + cb * 256, C_MR + (cb + 1) * 256))
        elif kind == 2:
            chunks["ms", cb] = _sigmoid(proj(C_MS + cb * 256, C_MS + (cb + 1) * 256))
        else:
            chunks["br", cb] = _dot(or_scr[...], wbr_ref[:, cb * 256:(cb + 1) * 256])

    assert nblk == 2
    pairs.sort(key=lambda cp: (cp[1], cp[0]))
    order = [0, 4, 8, 12, 1, 5, 9, 13, 2, 6, 10, 14, 3, 7, 11, 15]
    s2_next = qk(0)
    for i, (c, p) in enumerate(pairs):
        s2 = s2_next
        if i + 1 < len(pairs):
            s2_next = qk(i + 1)
        filler(order[i])
        es, invs = [], []
        for a in range(2):
            sink = sink_ref[2 * p + a]
            s = jnp.where(allowed[c], s2[:, a * 2 * WINDOW:(a + 1) * 2 * WINDOW], -jnp.inf)
            m = jnp.maximum(jnp.max(s, axis=-1, keepdims=True), sink)
            e = jnp.exp(s - m)
            invs.append(1.0 / (jnp.sum(e, axis=-1, keepdims=True) + jnp.exp(sink - m)))
            es.append(e.astype(BF16))
        o2 = _dot(jnp.concatenate(es, axis=1), vbds[(c, p // 2)])
        o2 = o2 * jnp.where(low, invs[0], invs[1])
        g = chunks["sg", p // 2][c * WINDOW:(c + 1) * WINDOW, (p % 2) * LANES:(p % 2 + 1) * LANES]
        os_scr[c * WINDOW:(c + 1) * WINDOW, p * LANES:(p + 1) * LANES] = (o2 * (g * _sigmoid(g))).astype(BF16)
    pk_scr[...] = sk[tm - WINDOW:]
    pv_scr[...] = sv[tm - WINDOW:]

    merged = []
    for cb in range(D_MODEL // 256):
        br_s = _dot(os_scr[...], wbs_ref[:, cb * 256:(cb + 1) * 256])
        merged.append((chunks["mr", cb] * chunks["br", cb] + chunks["ms", cb] * br_s).astype(BF16))
    y_ref[...] = x + _dot(jnp.concatenate(merged, axis=1), wo_ref[...])

    @pl.when(t == pl.num_programs(1) - 1)
    def _():
        st_ref[...] = s_scr[...]
        ko_ref[...] = sk[tm - WINDOW:]
        vo_ref[...] = sv[tm - WINDOW:]


def _const_spec(shape):
    nd = len(shape)
    return pl.BlockSpec(shape, lambda *_: (0,) * nd, pipeline_mode=pl.Buffered(1))


def _rope_tables(pos, d):
    inv = np.float64(ROPE_THETA) ** (-np.arange(0, d, 2, dtype=np.float64) / d)
    ang = pos.astype(np.float64)[:, None] * inv[None, :]
    c, s = np.cos(ang), np.sin(ang)
    reps = LANES // d
    cos = np.tile(np.concatenate([c, c], axis=1), (1, reps))
    sin = np.tile(np.concatenate([-s, s], axis=1), (1, reps))
    return cos.astype(np.float32), sin.astype(np.float32)


def _log_decay():
    return np.log(1.0 - 2.0 ** (-5.0 - np.arange(RET_HEADS, dtype=np.float64)))


def _group_matrix():
    g = np.arange(256) // SWA_HD
    return jnp.asarray((g[:, None] == g[None, :]).astype(np.float32) / SWA_HD, dtype=BF16)


def _prompt_layer(x, w, tm):
    B, T, D = x.shape
    nt = T // tm
    lg = _log_decay()
    n = np.arange(tm, dtype=np.float64)
    diff = n[:, None] - n[None, :]
    dmask = np.where(diff[None] >= 0, np.exp(np.maximum(diff, 0.0)[None] * lg[:, None, None]), 0.0).astype(np.float32)
    cross = np.broadcast_to(np.exp((n[None, :] + 1.0) * lg[:, None])[:, :, None], (RET_HEADS, tm, RET_DK))
    kdec = np.broadcast_to(np.exp((tm - 1.0 - n)[None, :] * lg[:, None])[:, :, None], (RET_HEADS, tm, RET_DK))
    decay_chunk = tuple(float(v) for v in np.exp(tm * lg))
    pos = np.arange(T)
    rcos, rsin = _rope_tables(pos, RET_DK)
    scos, ssin = _rope_tables(pos, SWA_HD)

    tok = lambda b, t: (b, t, 0)
    tab = lambda b, t: (t, 0)
    in_specs = [
        pl.BlockSpec((None, tm, D), tok),
        _const_spec((1, D)),
        _const_spec((D, IN_WIDTH)),
        _const_spec((1, RET_V)),
        _const_spec((1, 256)),
        _const_spec((1, 256)),
        pl.BlockSpec(memory_space=pltpu.SMEM),
        _const_spec((RET_V, D)),
        _const_spec((SWA_Q, D)),
        _const_spec((D, D)),
        pl.BlockSpec((tm, LANES), tab),
        pl.BlockSpec((tm, LANES), tab),
        pl.BlockSpec((tm, LANES), tab),
        pl.BlockSpec((tm, LANES), tab),
        _const_spec((RET_HEADS, tm, tm)),
        _const_spec((RET_HEADS, tm, RET_DK)),
        _const_spec((RET_HEADS, tm, RET_DK)),
        _const_spec((256, 256)),
    ]
    out_shape = (
        jax.ShapeDtypeStruct((B, T, D), F32),
        jax.ShapeDtypeStruct((B, RET_HEADS, RET_DK, RET_DV), F32),
        jax.ShapeDtypeStruct((B, WINDOW, SWA_KV), F32),
        jax.ShapeDtypeStruct((B, WINDOW, SWA_KV), F32),
    )
    out_specs = (
        pl.BlockSpec((None, tm, D), tok),
        pl.BlockSpec((None, RET_HEADS, RET_DK, RET_DV), lambda b, t: (b, 0, 0, 0)),
        pl.BlockSpec((None, WINDOW, SWA_KV), lambda b, t: (b, 0, 0)),
        pl.BlockSpec((None, WINDOW, SWA_KV), lambda b, t: (b, 0, 0)),
    )
    scratch = [
        pltpu.VMEM((RET_HEADS, RET_DK, RET_DV), F32),
        pltpu.VMEM((WINDOW, SWA_KV), F32),
        pltpu.VMEM((WINDOW, SWA_KV), F32),
        pltpu.VMEM((tm, RET_V), BF16),
        pltpu.VMEM((tm, SWA_Q), BF16),
    ]
    return pl.pallas_call(
        functools.partial(_prompt_kernel, decay_chunk=decay_chunk),
        grid=(B, nt),
        in_specs=in_specs,
        out_specs=out_specs,
        out_shape=out_shape,
        scratch_shapes=scratch,
        compiler_params=pltpu.CompilerParams(
            dimension_semantics=("arbitrary", "arbitrary"), vmem_limit_bytes=VMEM_LIMIT),
        name="prompt_layer",
    )(x, w["ng"], w["win"], w["rng"], w["qg"], w["kg"], w["sinks"], w["wbr"], w["wbs"], w["wo"],
      rcos, rsin, scos, ssin, dmask, cross.astype(np.float32), kdec.astype(np.float32), _group_matrix())


def _decode_kernel(x_ref, ng_ref, win_ref, rng_ref, qg_ref, kg_ref, sinkb_ref, wbr_ref, wbs_ref, wo_ref,
                   rcos_ref, rsin_ref, scos_ref, ssin_ref, gmat_ref, rept_ref, rep_ref, hmask_ref,
                   st_ref, kt_ref, vt_ref,
                   y_ref, sto_ref, kto_ref, vto_ref,
                   rq_scr, rk_scr, rv_scr, sq_scr, skt_scr, svt_scr, or_scr, os_scr, *, decay_step):
    i = pl.program_id(0)
    bs = st_ref.shape[0]
    lane = lax.broadcasted_iota(jnp.int32, (1, LANES), 1)
    first_half = (lane % SWA_HD) < (SWA_HD // 2)

    def proj(lo, hi):
        h = _rms_rows(x_ref[...], ng_ref[...]).astype(BF16)
        return _dot(h, win_ref[:, lo:hi])

    @pl.when(i == 0)
    def _():
        rcos = rcos_ref[...]
        rsin = rsin_ref[...]
        rq = proj(C_RQ, C_RK)
        rk = proj(C_RK, C_RV)
        for hd in range(RET_HEADS):
            sl = slice(hd * RET_DK, (hd + 1) * RET_DK)
            rq_scr[:, sl] = _rope128(rq[:, sl], rcos, rsin)
            rk_scr[:, sl] = _rope128(rk[:, sl], rcos, rsin) * (RET_DK ** -0.5)
        rv_scr[...] = proj(C_RV, C_RG)
        gmat = gmat_ref[...]
        scos = scos_ref[...]
        ssin = ssin_ref[...]
        for cb in range(SWA_Q // 256):
            sq_scr[:, cb * 256:(cb + 1) * 256] = _swa_norm_rope(
                proj(C_SQ + cb * 256, C_SQ + (cb + 1) * 256), gmat, qg_ref[...], scos, ssin,
                first_half) * (SWA_HD ** -0.5)
        skt_scr[...] = _swa_norm_rope(proj(C_SK, C_SV), gmat, kg_ref[...], scos, ssin, first_half).T
        svt_scr[...] = proj(C_SV, C_SG).T

    r0 = pl.multiple_of(i * bs, bs)
    rows = pl.ds(r0, bs)
    row8 = lax.broadcasted_iota(jnp.int32, (bs, 1), 0)
    q8 = rq_scr[rows, :]
    k8 = rk_scr[rows, :]
    v8 = rv_scr[rows, :]
    sq8 = sq_scr[rows, :]

    hmask = hmask_ref[...]
    lexp = jnp.concatenate([jnp.broadcast_to(sq8[s:s + 1, :], (SWA_HEADS, SWA_Q)) for s in range(bs)], axis=0)
    qexp = _dot((lexp * hmask).astype(BF16), rept_ref[...]).astype(BF16)

    shift = (LANES - r0) % LANES
    knew = pltpu.roll(skt_scr[...], shift, 1)
    vnew = pltpu.roll(svt_scr[...], shift, 1)
    newest = lane == WINDOW - 1
    scs, vts = [], []
    for s in range(bs):
        kt_new = jnp.where(newest, knew[:, s:s + 1], pltpu.roll(kt_ref[s], WINDOW - 1, 1))
        vt_new = jnp.where(newest, vnew[:, s:s + 1], pltpu.roll(vt_ref[s], WINDOW - 1, 1))
        kto_ref[s] = kt_new
        vto_ref[s] = vt_new
        scs.append(_dot(qexp[s * SWA_HEADS:(s + 1) * SWA_HEADS], kt_new.astype(BF16)))
        vts.append(vt_new.astype(BF16))

    o_acc = [jnp.zeros((bs, RET_DV), F32)] * RET_HEADS
    for s in range(bs):
        sel = row8 == s
        for hd in range(RET_HEADS):
            qh = q8[:, hd * RET_DK:(hd + 1) * RET_DK]
            kh = k8[:, hd * RET_DK:(hd + 1) * RET_DK]
            vh = v8[:, hd * RET_DV:(hd + 1) * RET_DV]
            state = st_ref[s, hd]
            inter = _dot(qh.astype(BF16), state.astype(BF16))
            outer = _dot_tn(jnp.where(sel, kh, 0.0).astype(BF16), vh.astype(BF16))
            sto_ref[s, hd] = decay_step[hd] * state + outer
            o_acc[hd] = jnp.where(sel, inter, o_acc[hd])
    for hd in range(RET_HEADS):
        qh = q8[:, hd * RET_DK:(hd + 1) * RET_DK]
        kh = k8[:, hd * RET_DK:(hd + 1) * RET_DK]
        vh = v8[:, hd * RET_DV:(hd + 1) * RET_DV]
        or_scr[rows, hd * RET_DV:(hd + 1) * RET_DV] = (
            jnp.sum(qh * kh, axis=-1, keepdims=True) * vh + decay_step[hd] * o_acc[hd])

    sink_col = sinkb_ref[:, 0:1]
    outs = []
    for s in range(bs):
        m = jnp.maximum(jnp.max(scs[s], axis=-1, keepdims=True), sink_col)
        e = jnp.exp(scs[s] - m)
        p = e / (jnp.sum(e, axis=-1, keepdims=True) + jnp.exp(sink_col - m))
        outs.append(_dot_nt(p.astype(BF16), vts[s]))
    oblk = jnp.concatenate(outs, axis=0)
    hi = oblk.astype(BF16)
    lo = (oblk - hi.astype(F32)).astype(BF16)
    onat = (_dot(hi, rep_ref[...]) + _dot(lo, rep_ref[...])) * hmask
    os_scr[rows, :] = jnp.sum(onat.reshape(bs, SWA_HEADS, SWA_Q), axis=1)

    @pl.when(i == pl.num_programs(0) - 1)
    def _():
        rg = proj(C_RG, C_SQ)
        o_r = or_scr[...]
        parts = []
        for hd in range(RET_HEADS):
            sl = slice(hd * RET_DV, (hd + 1) * RET_DV)
            parts.append(_rms_rows(o_r[:, sl], rng_ref[:, sl]) * (rg[:, sl] * _sigmoid(rg[:, sl])))
        br_r = _dot(jnp.concatenate(parts, axis=1).astype(BF16), wbr_ref[...])
        sg = proj(C_SG, C_MR)
        br_s = _dot((os_scr[...] * (sg * _sigmoid(sg))).astype(BF16), wbs_ref[...])
        merged = _sigmoid(proj(C_MR, C_MS)) * br_r + _sigmoid(proj(C_MS, IN_WIDTH)) * br_s
        y_ref[...] = x_ref[...] + _dot(merged.astype(BF16), wo_ref[...])


def _decode_layer(x, state, cache_kt, cache_vt, w, past_len, bs):
    nb, D = x.shape
    nsteps = nb // bs
    decay_step = tuple(float(v) for v in np.exp(_log_decay()))
    pos = np.full((1,), past_len)
    rcos, rsin = _rope_tables(pos, RET_DK)
    scos, ssin = _rope_tables(pos, SWA_HD)
    lane_kv = (np.arange(SWA_Q) // SWA_HD) // (SWA_HEADS // SWA_KV_HEADS)
    lane_d = np.arange(SWA_Q) % SWA_HD
    rep = np.zeros((SWA_KV, SWA_Q), np.float32)
    rep[lane_kv * SWA_HD + lane_d, np.arange(SWA_Q)] = 1.0
    hmask = np.tile((np.arange(SWA_Q)[None, :] // SWA_HD == np.arange(SWA_HEADS)[:, None]).astype(np.float32), (bs, 1))
    sinkb = jnp.broadcast_to(w["sinks"][:, None], (SWA_HEADS, LANES))

    blk4 = lambda i: (i, 0, 0, 0)
    blk3 = lambda i: (i, 0, 0)
    in_specs = [
        _const_spec((nb, D)),
        _const_spec((1, D)),
        _const_spec((D, IN_WIDTH)),
        _const_spec((1, RET_V)),
        _const_spec((1, 256)),
        _const_spec((1, 256)),
        _const_spec((SWA_HEADS, LANES)),
        _const_spec((RET_V, D)),
        _const_spec((SWA_Q, D)),
        _const_spec((D, D)),
        _const_spec((1, LANES)),
        _const_spec((1, LANES)),
        _const_spec((1, LANES)),
        _const_spec((1, LANES)),
        _const_spec((256, 256)),
        _const_spec((SWA_Q, SWA_KV)),
        _const_spec((SWA_KV, SWA_Q)),
        _const_spec((bs * SWA_HEADS, SWA_Q)),
        pl.BlockSpec((bs, RET_HEADS, RET_DK, RET_DV), blk4),
        pl.BlockSpec((bs, SWA_KV, WINDOW), blk3),
        pl.BlockSpec((bs, SWA_KV, WINDOW), blk3),
    ]
    out_shape = (
        jax.ShapeDtypeStruct((nb, D), F32),
        jax.ShapeDtypeStruct(state.shape, F32),
        jax.ShapeDtypeStruct(cache_kt.shape, F32),
        jax.ShapeDtypeStruct(cache_vt.shape, F32),
    )
    out_specs = (
        pl.BlockSpec((nb, D), lambda i: (0, 0)),
        pl.BlockSpec((bs, RET_HEADS, RET_DK, RET_DV), blk4),
        pl.BlockSpec((bs, SWA_KV, WINDOW), blk3),
        pl.BlockSpec((bs, SWA_KV, WINDOW), blk3),
    )
    scratch = [
        pltpu.VMEM((nb, RET_QK), F32),
        pltpu.VMEM((nb, RET_QK), F32),
        pltpu.VMEM((nb, RET_V), F32),
        pltpu.VMEM((nb, SWA_Q), F32),
        pltpu.VMEM((SWA_KV, nb), F32),
        pltpu.VMEM((SWA_KV, nb), F32),
        pltpu.VMEM((nb, RET_V), F32),
        pltpu.VMEM((nb, SWA_Q), F32),
    ]
    return pl.pallas_call(
        functools.partial(_decode_kernel, decay_step=decay_step),
        grid=(nsteps,),
        in_specs=in_specs,
        out_specs=out_specs,
        out_shape=out_shape,
        scratch_shapes=scratch,
        compiler_params=pltpu.CompilerParams(dimension_semantics=("arbitrary",), vmem_limit_bytes=VMEM_LIMIT),
        name="decode_layer",
    )(x, w["ng"], w["win"], w["rng"], w["qg"], w["kg"], sinkb, w["wbr"], w["wbs"], w["wo"],
      rcos, rsin, scos, ssin, _group_matrix(), jnp.asarray(rep.T, BF16), jnp.asarray(rep, BF16), hmask,
      state, cache_kt, cache_vt)


def kernel(x_prompt, x_sample, state_ret, cache_swa_k, cache_swa_v, norm_g, w_in, ret_norm_g, swa_q_g, swa_k_g,
           swa_sinks, w_br_ret, w_br_swa, w_out):
    depth = norm_g.shape[0]
    assert depth == 1 and x_sample.shape[1] == 1
    B, T, D = x_prompt.shape
    nb = x_sample.shape[0]
    wc = cache_swa_k.shape[2]
    assert wc == WINDOW and T % PROMPT_TM == 0 and nb % DEC_BS == 0
    l = 0
    w = {
        "ng": norm_g[l][None, :],
        "win": w_in[l].astype(BF16),
        "rng": ret_norm_g[l].reshape(1, RET_V),
        "qg": jnp.tile(swa_q_g[l], 256 // SWA_HD)[None, :],
        "kg": jnp.tile(swa_k_g[l], 256 // SWA_HD)[None, :],
        "sinks": swa_sinks[l],
        "wbr": w_br_ret[l].astype(BF16),
        "wbs": w_br_swa[l].astype(BF16),
        "wo": w_out[l].astype(BF16),
    }
    yp, rp, kp, vp = _prompt_layer(x_prompt, w, PROMPT_TM)
    to_t = lambda c: jnp.transpose(c.reshape(nb, wc, SWA_KV), (0, 2, 1))
    from_t = lambda c: jnp.transpose(c, (0, 2, 1)).reshape(1, nb, wc, SWA_KV_HEADS, SWA_HD)
    ys, rs, ks, vs = _decode_layer(x_sample[:, 0, :], state_ret[l], to_t(cache_swa_k[l]), to_t(cache_swa_v[l]),
                                   w, PAST_LEN, DEC_BS)
    return (yp, ys[:, None, :], rp[None], rs[None],
            kp.reshape(1, B, WINDOW, SWA_KV_HEADS, SWA_HD), vp.reshape(1, B, WINDOW, SWA_KV_HEADS, SWA_HD),
            from_t(ks), from_t(vs))
```

```python
import functools

import numpy as np
import jax
import jax.numpy as jnp
from jax import lax
from jax.experimental import pallas as pl
from jax.experimental.pallas import tpu as pltpu

F32 = jnp.float32
BF16 = jnp.bfloat16

D_MODEL = 1024
RET_HEADS = 4
RET_DK = 128
RET_DV = 256
RET_QK = RET_HEADS * RET_DK
RET_V = RET_HEADS * RET_DV
SWA_HEADS = 16
SWA_KV_HEADS = 4
SWA_HD = 64
SWA_Q = SWA_HEADS * SWA_HD
SWA_KV = SWA_KV_HEADS * SWA_HD
WINDOW = 128
ROPE_THETA = 10000.0
EPS = 1e-6
PAST_LEN = 8192

C_RQ = 0
C_RK = C_RQ + RET_QK
C_RV = C_RK + RET_QK
C_RG = C_RV + RET_V
C_SQ = C_RG + RET_V
C_SK = C_SQ + SWA_Q
C_SV = C_SK + SWA_KV
C_SG = C_SV + SWA_KV
C_MR = C_SG + SWA_Q
C_MS = C_MR + D_MODEL
IN_WIDTH = C_MS + D_MODEL

LANES = 128
SUBLANES = 8
PROMPT_TM = 256
VMEM_LIMIT = 60 * 1024 * 1024


def _dot(a, b):
    return jnp.dot(a, b, preferred_element_type=F32)


def _dot_nt(a, b):
    return lax.dot_general(a, b, (((1,), (1,)), ((), ())), preferred_element_type=F32)


def _dot_tn(a, b):
    return lax.dot_general(a, b, (((0,), (0,)), ((), ())), preferred_element_type=F32)


def _sigmoid(x):
    return 1.0 / (1.0 + jnp.exp(-x))


def _rms_rows(x, g):
    return x * lax.rsqrt(jnp.mean(x * x, axis=-1, keepdims=True) + EPS) * g


def _group_mean_sq(x, gmat):
    x2 = x * x
    hi = x2.astype(BF16)
    lo = (x2 - hi.astype(F32)).astype(BF16)
    return _dot(hi, gmat) + _dot(lo, gmat)


def _rope128(x, cos, sin_signed):
    return x * cos + pltpu.roll(x, 64, 1) * sin_signed


def _rope64(x, cos, sin_signed, first_half):
    rot = jnp.where(first_half, pltpu.roll(x, 96, 1), pltpu.roll(x, 32, 1))
    return x * cos + rot * sin_signed


def _swa_norm_rope(x, gmat, gain, cos, sin_signed, first_half):
    xn = x * lax.rsqrt(_group_mean_sq(x, gmat) + EPS) * gain
    cols = [_rope64(xn[:, c * LANES:(c + 1) * LANES], cos, sin_signed, first_half) for c in range(2)]
    return jnp.concatenate(cols, axis=1)


def _heads_to_rows(qrow):
    row = lax.broadcasted_iota(jnp.int32, (SUBLANES, LANES), 0)
    half = lax.broadcasted_iota(jnp.int32, (SUBLANES, LANES), 1) // SWA_HD
    zeros = jnp.zeros((SUBLANES, LANES), F32)
    tiles = []
    for jt in range(2):
        acc = zeros
        for r in range(8):
            sc = 4 * jt + r // 2
            piece = jnp.broadcast_to(qrow[:, sc * LANES:(sc + 1) * LANES], (SUBLANES, LANES))
            if r % 2 != r // 4:
                piece = pltpu.roll(piece, SWA_HD, 1)
            acc = jnp.where((row == r) & (half == r // 4), piece, acc)
        tiles.append(acc)
    return jnp.concatenate([jnp.concatenate([tiles[0], zeros], axis=1),
                            jnp.concatenate([zeros, tiles[1]], axis=1)], axis=0)


def _rows_to_heads(o):
    low = lax.broadcasted_iota(jnp.int32, (1, LANES), 1) < SWA_HD
    cols = []
    for tc in range(SWA_HEADS // 2):
        pieces = []
        for e in range(2):
            h = 2 * tc + e
            j = h // 4
            piece = o[h:h + 1, (j // 2) * LANES:(j // 2 + 1) * LANES]
            if j % 2 != e:
                piece = pltpu.roll(piece, SWA_HD, 1)
            pieces.append(piece)
        cols.append(jnp.where(low, pieces[0], pieces[1]))
    return jnp.concatenate(cols, axis=1)


def _layer_kernel(x_ref, ng_ref, win_ref, rng_ref, qg_ref, kg_ref, sink_ref, wbr_ref, wbs_ref, wo_ref,
                  rcos_ref, rsin_ref, scos_ref, ssin_ref, dmask_ref, cross_ref, kdec_ref, gmat_ref,
                  xs_ref, sinkb_ref, drcos_ref, drsin_ref, dscos_ref, dssin_ref, dst_ref, dkt_ref, dvt_ref,
                  y_ref, st_ref, ko_ref, vo_ref, ys_ref, dsto_ref, dkto_ref, dvto_ref,
                  s_scr, pk_scr, pv_scr, or_scr, os_scr,
                  drq_scr, drk_scr, drv_scr, dsq_scr, dsk_scr, dsv_scr, dor_scr, dos_scr,
                  *, decay_chunk, decay_step):
    t = pl.program_id(1)
    step = pl.program_id(0) * pl.num_programs(1) + t
    last_step = pl.num_programs(0) * pl.num_programs(1) - 1
    tm = x_ref.shape[0]
    nblk = tm // WINDOW
    nseq = dst_ref.shape[0]

    gmat = gmat_ref[...]
    lane = lax.broadcasted_iota(jnp.int32, (1, LANES), 1)
    first_half = (lane % SWA_HD) < (SWA_HD // 2)
    low = lane < SWA_HD

    def dproj(lo, hi):
        hs = _rms_rows(xs_ref[...], ng_ref[...]).astype(BF16)
        return _dot(hs, win_ref[:, lo:hi])

    @pl.when(step == 0)
    def _():
        drcos = drcos_ref[...]
        drsin = drsin_ref[...]
        rq = dproj(C_RQ, C_RK)
        rk = dproj(C_RK, C_RV)
        for hd in range(RET_HEADS):
            sl = slice(hd * RET_DK, (hd + 1) * RET_DK)
            drq_scr[:, sl] = _rope128(rq[:, sl], drcos, drsin)
            drk_scr[:, sl] = _rope128(rk[:, sl], drcos, drsin) * (RET_DK ** -0.5)
        drv_scr[...] = dproj(C_RV, C_RG)
        dscos = dscos_ref[...]
        dssin = dssin_ref[...]
        for cb in range(SWA_Q // 256):
            dsq_scr[:, cb * 256:(cb + 1) * 256] = _swa_norm_rope(
                dproj(C_SQ + cb * 256, C_SQ + (cb + 1) * 256), gmat, qg_ref[...], dscos, dssin,
                first_half) * (SWA_HD ** -0.5)
        dsk_scr[...] = _swa_norm_rope(dproj(C_SK, C_SV), gmat, kg_ref[...], dscos, dssin, first_half)
        dsv_scr[...] = dproj(C_SV, C_SG)

    @pl.when(t == 0)
    def _():
        s_scr[...] = jnp.zeros_like(s_scr)
        pk_scr[...] = jnp.zeros_like(pk_scr)
        pv_scr[...] = jnp.zeros_like(pv_scr)

    seq0 = step * nseq
    grp = pl.multiple_of((seq0 // SUBLANES) * SUBLANES, SUBLANES)
    rows8 = pl.ds(grp, SUBLANES)
    row8 = lax.broadcasted_iota(jnp.int32, (SUBLANES, 1), 0)
    newest = lane == WINDOW - 1
    drop_oldest = WINDOW - 1 + jnp.minimum(step, 0)
    dstate = {}

    def column_broadcast(rows_scr):
        rowi = lax.broadcasted_iota(jnp.int32, (SUBLANES, SWA_KV), 0)
        ones_row = lax.broadcasted_iota(jnp.int32, (SUBLANES, nseq * LANES), 0)
        ones_lane = lax.broadcasted_iota(jnp.int32, (SUBLANES, nseq * LANES), 1)
        tile = jnp.zeros((SUBLANES, SWA_KV), F32)
        for s in range(nseq):
            r = rows_scr[pl.ds(seq0 + s, 1), :]
            hi = r.astype(BF16).astype(F32)
            mid = (r - hi).astype(BF16).astype(F32)
            lo = (r - hi) - mid
            for i, term in enumerate((hi, mid, lo)):
                tile = jnp.where(rowi == 3 * s + i, jnp.broadcast_to(term, (SUBLANES, SWA_KV)), tile)
        pick = (ones_row // 3 == ones_lane // LANES) & (ones_row < 3 * nseq)
        return _dot_tn(tile.astype(BF16), jnp.where(pick, 1.0, 0.0).astype(BF16))

    def decode_scores():
        q8 = drq_scr[rows8, :]
        k8 = drk_scr[rows8, :]
        v8 = drv_scr[rows8, :]
        mine = (row8 >= seq0 - grp) & (row8 < seq0 - grp + nseq)
        o_acc = [jnp.zeros((SUBLANES, RET_DV), F32)] * RET_HEADS
        knew = column_broadcast(dsk_scr)
        vnew = column_broadcast(dsv_scr)
        scs, vts = [], []
        for s in range(nseq):
            b = seq0 + s
            sel = row8 == (b - grp)
            for hd in range(RET_HEADS):
                qh = q8[:, hd * RET_DK:(hd + 1) * RET_DK]
                kh = k8[:, hd * RET_DK:(hd + 1) * RET_DK]
                vh = v8[:, hd * RET_DV:(hd + 1) * RET_DV]
                state = dst_ref[s, hd]
                inter = _dot(qh.astype(BF16), state.astype(BF16))
                outer = _dot_tn(jnp.where(sel, kh, 0.0).astype(BF16), vh.astype(BF16))
                dsto_ref[s, hd] = decay_step[hd] * state + outer
                o_acc[hd] = jnp.where(sel, inter, o_acc[hd])
            kt_new = jnp.where(newest, knew[:, s * LANES:(s + 1) * LANES], pltpu.roll(dkt_ref[s], drop_oldest, 1))
            vt_new = jnp.where(newest, vnew[:, s * LANES:(s + 1) * LANES], pltpu.roll(dvt_ref[s], drop_oldest, 1))
            dkto_ref[s] = kt_new
            dvto_ref[s] = vt_new
            qexp = _heads_to_rows(dsq_scr[pl.ds(b, 1), :]).astype(BF16)
            scs.append(_dot(qexp, kt_new.astype(BF16)))
            vts.append(vt_new.astype(BF16))
        for hd in range(RET_HEADS):
            qh = q8[:, hd * RET_DK:(hd + 1) * RET_DK]
            kh = k8[:, hd * RET_DK:(hd + 1) * RET_DK]
            vh = v8[:, hd * RET_DV:(hd + 1) * RET_DV]
            sl = slice(hd * RET_DV, (hd + 1) * RET_DV)
            o = jnp.sum(qh * kh, axis=-1, keepdims=True) * vh + decay_step[hd] * o_acc[hd]
            dor_scr[rows8, sl] = jnp.where(mine, o, dor_scr[rows8, sl])
        dstate["scs"], dstate["vts"] = scs, vts

    def decode_attend():
        sink_col = sinkb_ref[:, 0:1]
        outs = []
        for s in range(nseq):
            sc = dstate["scs"][s]
            m = jnp.maximum(jnp.max(sc, axis=-1, keepdims=True), sink_col)
            e = jnp.exp(sc - m)
            p = e / (jnp.sum(e, axis=-1, keepdims=True) + jnp.exp(sink_col - m))
            pt = jnp.concatenate([p.astype(BF16), jnp.zeros((LANES - SWA_HEADS, WINDOW), BF16)], axis=0)
            out_t = _dot_nt(dstate["vts"][s], pt)
            outs.append(out_t.T[:SWA_HEADS])
        dstate["outs"] = outs

    def decode_store():
        for s in range(nseq):
            dos_scr[pl.ds(seq0 + s, 1), :] = _rows_to_heads(dstate["outs"][s])

    x = x_ref[...]
    h = _rms_rows(x, ng_ref[...]).astype(BF16)

    def proj(lo, hi):
        return _dot(h, win_ref[:, lo:hi])

    scos = scos_ref[...]
    ssin = ssin_ref[...]

    rq = proj(C_RQ, C_RK)
    decode_scores()
    rk = proj(C_RK, C_RV)
    rv = proj(C_RV, C_RG).astype(BF16)
    decode_attend()
    rcos = rcos_ref[...]
    rsin = rsin_ref[...]
    k_scale = RET_DK ** -0.5
    scores, inter, vs = [], [], []
    for hd in range(RET_HEADS):
        q = _rope128(rq[:, hd * RET_DK:(hd + 1) * RET_DK], rcos, rsin)
        k = _rope128(rk[:, hd * RET_DK:(hd + 1) * RET_DK], rcos, rsin) * k_scale
        v = rv[:, hd * RET_DV:(hd + 1) * RET_DV]
        state = s_scr[hd]
        scores.append(_dot_nt(q.astype(BF16), k.astype(BF16)))
        inter.append(_dot((q * cross_ref[hd]).astype(BF16), state.astype(BF16)))
        s_scr[hd] = decay_chunk[hd] * state + _dot_tn((k * kdec_ref[hd]).astype(BF16), v)
        vs.append(v)
    decode_store()
    sq_cols = []
    sk = sv = None
    for hd in range(RET_HEADS):
        g = proj(C_RG + hd * RET_DV, C_RG + (hd + 1) * RET_DV)
        sq_cols.append(_swa_norm_rope(proj(C_SQ + hd * 256, C_SQ + (hd + 1) * 256), gmat, qg_ref[...], scos, ssin,
                                      first_half) * (SWA_HD ** -0.5))
        if hd == 0:
            sk = _swa_norm_rope(proj(C_SK, C_SV), gmat, kg_ref[...], scos, ssin, first_half)
        if hd == 1:
            sv = proj(C_SV, C_SG)
        o = _dot((scores[hd] * dmask_ref[hd]).astype(BF16), vs[hd]) + inter[hd]
        on = _rms_rows(o, rng_ref[:, hd * RET_DV:(hd + 1) * RET_DV])
        or_scr[:, hd * RET_DV:(hd + 1) * RET_DV] = (on * (g * _sigmoid(g))).astype(BF16)

    row_i = lax.broadcasted_iota(jnp.int32, (WINDOW, 2 * WINDOW), 0)
    col_j = lax.broadcasted_iota(jnp.int32, (WINDOW, 2 * WINDOW), 1)
    allowed, kbds, vbds = [], {}, {}
    for c in range(nblk):
        lower = row_i + 1
        if c == 0:
            lower = jnp.maximum(lower, jnp.where(t == 0, WINDOW, 0))
        allowed.append((col_j >= lower) & (col_j <= row_i + WINDOW))

    def block_diag(c, j, src, prev_scr):
        r0 = c * WINDOW
        prev = prev_scr[...] if c == 0 else src[r0 - WINDOW:r0]
        cat = jnp.concatenate([prev, src[r0:r0 + WINDOW]], axis=0)
        col, half = divmod(j, 2)
        a = cat[:, col * LANES:(col + 1) * LANES]
        ar = pltpu.roll(a, SWA_HD, 1)
        if half == 0:
            bd = jnp.concatenate([jnp.where(low, a, 0.0), jnp.where(low, 0.0, ar)], axis=0)
        else:
            bd = jnp.concatenate([jnp.where(low, ar, 0.0), jnp.where(low, 0.0, a)], axis=0)
        return bd.astype(BF16)

    pairs = [(c, p) for c in range(nblk) for p in range(SWA_HEADS // 2)]

    def qk(i):
        c, p = pairs[i]
        j = p // 2
        if (c, j) not in kbds:
            kbds[(c, j)] = block_diag(c, j, sk, pk_scr)
            vbds[(c, j)] = block_diag(c, j, sv, pv_scr)
        qp = sq_cols[p // 2][c * WINDOW:(c + 1) * WINDOW, (p % 2) * LANES:(p % 2 + 1) * LANES].astype(BF16)
        return _dot_nt(qp, kbds[(c, j)])

    chunks = {}

    def filler(i):
        kind, cb = divmod(i, 4)
        if kind == 0:
            chunks["sg", cb] = proj(C_SG + cb * 256, C_SG + (cb + 1) * 256)
        elif kind == 1:
            chunks["mr", cb] = _sigmoid(proj(C_MR + cb * 256, C_MR + (cb + 1) * 256))
        elif kind == 2:
            chunks["ms", cb] = _sigmoid(proj(C_MS + cb * 256, C_MS + (cb + 1) * 256))
        else:
            chunks["br", cb] = _dot(or_scr[...], wbr_ref[:, cb * 256:(cb + 1) * 256])

    assert nblk == 2
    pairs.sort(key=lambda cp: (cp[1], cp[0]))
    order = [0, 4, 8, 12, 1, 5, 9, 13, 2, 6, 10, 14, 3, 7, 11, 15]
    s2_next = qk(0)
    for i, (c, p) in enumerate(pairs):
        s2 = s2_next
        if i + 1 < len(pairs):
            s2_next = qk(i + 1)
        filler(order[i])
        es, invs = [], []
        for a in range(2):
            sink = sink_ref[2 * p + a]
            s = jnp.where(allowed[c], s2[:, a * 2 * WINDOW:(a + 1) * 2 * WINDOW], -jnp.inf)
            m = jnp.maximum(jnp.max(s, axis=-1, keepdims=True), sink)
            e = jnp.exp(s - m)
            invs.append(1.0 / (jnp.sum(e, axis=-1, keepdims=True) + jnp.exp(sink - m)))
            es.append(e.astype(BF16))
        o2 = _dot(jnp.concatenate(es, axis=1), vbds[(c, p // 2)])
        o2 = o2 * jnp.where(low, invs[0], invs[1])
        g = chunks["sg", p // 2][c * WINDOW:(c + 1) * WINDOW, (p % 2) * LANES:(p % 2 + 1) * LANES]
        os_scr[c * WINDOW:(c + 1) * WINDOW, p * LANES:(p + 1) * LANES] = (o2 * (g * _sigmoid(g))).astype(BF16)
    pk_scr[...] = sk[tm - WINDOW:]
    pv_scr[...] = sv[tm - WINDOW:]

    merged = []
    for cb in range(D_MODEL // 256):
        br_s = _dot(os_scr[...], wbs_ref[:, cb * 256:(cb + 1) * 256])
        merged.append((chunks["mr", cb] * chunks["br", cb] + chunks["ms", cb] * br_s).astype(BF16))
    y_ref[...] = x + _dot(jnp.concatenate(merged, axis=1), wo_ref[...])

    @pl.when(t == pl.num_programs(1) - 1)
    def _():
        st_ref[...] = s_scr[...]
        ko_ref[...] = sk[tm - WINDOW:]
        vo_ref[...] = sv[tm - WINDOW:]

    @pl.when(step == last_step)
    def _():
        rg = dproj(C_RG, C_SQ)
        o_r = dor_scr[...]
        parts = []
        for hd in range(RET_HEADS):
            sl = slice(hd * RET_DV, (hd + 1) * RET_DV)
            parts.append(_rms_rows(o_r[:, sl], rng_ref[:, sl]) * (rg[:, sl] * _sigmoid(rg[:, sl])))
        br_r = _dot(jnp.concatenate(parts, axis=1).astype(BF16), wbr_ref[...])
        sg = dproj(C_SG, C_MR)
        br_s = _dot((dos_scr[...] * (sg * _sigmoid(sg))).astype(BF16), wbs_ref[...])
        mrg = _sigmoid(dproj(C_MR, C_MS)) * br_r + _sigmoid(dproj(C_MS, IN_WIDTH)) * br_s
        ys_ref[...] = xs_ref[...] + _dot(mrg.astype(BF16), wo_ref[...])


def _const_spec(shape):
    nd = len(shape)
    return pl.BlockSpec(shape, lambda *_: (0,) * nd, pipeline_mode=pl.Buffered(1))


def _rope_tables(pos, d):
    inv = np.float64(ROPE_THETA) ** (-np.arange(0, d, 2, dtype=np.float64) / d)
    ang = pos.astype(np.float64)[:, None] * inv[None, :]
    c, s = np.cos(ang), np.sin(ang)
    reps = LANES // d
    cos = np.tile(np.concatenate([c, c], axis=1), (1, reps))
    sin = np.tile(np.concatenate([-s, s], axis=1), (1, reps))
    return cos.astype(np.float32), sin.astype(np.float32)


def _log_decay():
    return np.log(1.0 - 2.0 ** (-5.0 - np.arange(RET_HEADS, dtype=np.float64)))


def _group_matrix():
    g = np.arange(256) // SWA_HD
    return jnp.asarray((g[:, None] == g[None, :]).astype(np.float32) / SWA_HD, dtype=BF16)


def _fused_layer(x, xs, state, cache_kt, cache_vt, w, tm):
    B, T, D = x.shape
    nt = T // tm
    nsteps = B * nt
    nb = xs.shape[0]
    assert nb % nsteps == 0 and nb % SUBLANES == 0
    nseq = nb // nsteps
    assert SUBLANES % nseq == 0 and 3 * nseq <= SUBLANES

    lg = _log_decay()
    n = np.arange(tm, dtype=np.float64)
    diff = n[:, None] - n[None, :]
    dmask = np.where(diff[None] >= 0, np.exp(np.maximum(diff, 0.0)[None] * lg[:, None, None]), 0.0).astype(np.float32)
    cross = np.broadcast_to(np.exp((n[None, :] + 1.0) * lg[:, None])[:, :, None], (RET_HEADS, tm, RET_DK))
    kdec = np.broadcast_to(np.exp((tm - 1.0 - n)[None, :] * lg[:, None])[:, :, None], (RET_HEADS, tm, RET_DK))
    decay_chunk = tuple(float(v) for v in np.exp(tm * lg))
    decay_step = tuple(float(v) for v in np.exp(lg))
    pos = np.arange(T)
    rcos, rsin = _rope_tables(pos, RET_DK)
    scos, ssin = _rope_tables(pos, SWA_HD)
    dpos = np.full((1,), PAST_LEN)
    drcos, drsin = _rope_tables(dpos, RET_DK)
    dscos, dssin = _rope_tables(dpos, SWA_HD)
    sinkb = jnp.broadcast_to(w["sinks"][:, None], (SWA_HEADS, LANES))

    tok = lambda b, t: (b, t, 0)
    tab = lambda b, t: (t, 0)
    seq4 = lambda b, t: (b * nt + t, 0, 0, 0)
    seq3 = lambda b, t: (b * nt + t, 0, 0)
    in_specs = [
        pl.BlockSpec((None, tm, D), tok),
        _const_spec((1, D)),
        _const_spec((D, IN_WIDTH)),
        _const_spec((1, RET_V)),
        _const_spec((1, 256)),
        _const_spec((1, 256)),
        pl.BlockSpec(memory_space=pltpu.SMEM),
        _const_spec((RET_V, D)),
        _const_spec((SWA_Q, D)),
        _const_spec((D, D)),
        pl.BlockSpec((tm, LANES), tab),
        pl.BlockSpec((tm, LANES), tab),
        pl.BlockSpec((tm, LANES), tab),
        pl.BlockSpec((tm, LANES), tab),
        _const_spec((RET_HEADS, tm, tm)),
        _const_spec((RET_HEADS, tm, RET_DK)),
        _const_spec((RET_HEADS, tm, RET_DK)),
        _const_spec((256, 256)),
        _const_spec((nb, D)),
        _const_spec((SWA_HEADS, LANES)),
        _const_spec((1, LANES)),
        _const_spec((1, LANES)),
        _const_spec((1, LANES)),
        _const_spec((1, LANES)),
        pl.BlockSpec((nseq, RET_HEADS, RET_DK, RET_DV), seq4),
        pl.BlockSpec((nseq, SWA_KV, WINDOW), seq3),
        pl.BlockSpec((nseq, SWA_KV, WINDOW), seq3),
    ]
    out_shape = (
        jax.ShapeDtypeStruct((B, T, D), F32),
        jax.ShapeDtypeStruct((B, RET_HEADS, RET_DK, RET_DV), F32),
        jax.ShapeDtypeStruct((B, WINDOW, SWA_KV), F32),
        jax.ShapeDtypeStruct((B, WINDOW, SWA_KV), F32),
        jax.ShapeDtypeStruct((nb, D), F32),
        jax.ShapeDtypeStruct(state.shape, F32),
        jax.ShapeDtypeStruct(cache_kt.shape, F32),
        jax.ShapeDtypeStruct(cache_vt.shape, F32),
    )
    out_specs = (
        pl.BlockSpec((None, tm, D), tok),
        pl.BlockSpec((None, RET_HEADS, RET_DK, RET_DV), lambda b, t: (b, 0, 0, 0)),
        pl.BlockSpec((None, WINDOW, SWA_KV), lambda b, t: (b, 0, 0)),
        pl.BlockSpec((None, WINDOW, SWA_KV), lambda b, t: (b, 0, 0)),
        pl.BlockSpec((nb, D), lambda b, t: (0, 0)),
        pl.BlockSpec((nseq, RET_HEADS, RET_DK, RET_DV), seq4),
        pl.BlockSpec((nseq, SWA_KV, WINDOW), seq3),
        pl.BlockSpec((nseq, SWA_KV, WINDOW), seq3),
    )
    scratch = [
        pltpu.VMEM((RET_HEADS, RET_DK, RET_DV), F32),
        pltpu.VMEM((WINDOW, SWA_KV), F32),
        pltpu.VMEM((WINDOW, SWA_KV), F32),
        pltpu.VMEM((tm, RET_V), BF16),
        pltpu.VMEM((tm, SWA_Q), BF16),
        pltpu.VMEM((nb, RET_QK), F32),
        pltpu.VMEM((nb, RET_QK), F32),
        pltpu.VMEM((nb, RET_V), F32),
        pltpu.VMEM((nb, SWA_Q), F32),
        pltpu.VMEM((nb, SWA_KV), F32),
        pltpu.VMEM((nb, SWA_KV), F32),
        pltpu.VMEM((nb, RET_V), F32),
        pltpu.VMEM((nb, SWA_Q), F32),
    ]
    return pl.pallas_call(
        functools.partial(_layer_kernel, decay_chunk=decay_chunk, decay_step=decay_step),
        grid=(B, nt),
        in_specs=in_specs,
        out_specs=out_specs,
        out_shape=out_shape,
        scratch_shapes=scratch,
        compiler_params=pltpu.CompilerParams(
            dimension_semantics=("arbitrary", "arbitrary"), vmem_limit_bytes=VMEM_LIMIT),
        name="hybrid_layer",
    )(x, w["ng"], w["win"], w["rng"], w["qg"], w["kg"], w["sinks"], w["wbr"], w["wbs"], w["wo"],
      rcos, rsin, scos, ssin, dmask, cross.astype(np.float32), kdec.astype(np.float32), _group_matrix(),
      xs, sinkb, drcos, drsin, dscos, dssin, state, cache_kt, cache_vt)


def kernel(x_prompt, x_sample, state_ret, cache_swa_k, cache_swa_v, norm_g, w_in, ret_norm_g, swa_q_g, swa_k_g,
           swa_sinks, w_br_ret, w_br_swa, w_out):
    depth = norm_g.shape[0]
    assert depth == 1 and x_sample.shape[1] == 1
    B, T, D = x_prompt.shape
    nb = x_sample.shape[0]
    wc = cache_swa_k.shape[2]
    assert wc == WINDOW and nb == LANES and T % PROMPT_TM == 0
    l = 0
    w = {
        "ng": norm_g[l][None, :],
        "win": w_in[l].astype(BF16),
        "rng": ret_norm_g[l].reshape(1, RET_V),
        "qg": jnp.tile(swa_q_g[l], 256 // SWA_HD)[None, :],
        "kg": jnp.tile(swa_k_g[l], 256 // SWA_HD)[None, :],
        "sinks": swa_sinks[l],
        "wbr": w_br_ret[l].astype(BF16),
        "wbs": w_br_swa[l].astype(BF16),
        "wo": w_out[l].astype(BF16),
    }
    to_t = lambda c: jnp.transpose(c.reshape(nb, wc, SWA_KV), (0, 2, 1))
    from_t = lambda c: jnp.transpose(c, (0, 2, 1)).reshape(1, nb, wc, SWA_KV_HEADS, SWA_HD)
    yp, rp, kp, vp, ys, rs, ks, vs = _fused_layer(
        x_prompt, x_sample[:, 0, :], state_ret[l], to_t(cache_swa_k[l]), to_t(cache_swa_v[l]), w, PROMPT_TM)
    return (yp, ys[:, None, :], rp[None], rs[None],
            kp.reshape(1, B, WINDOW, SWA_KV_HEADS, SWA_HD), vp.reshape(1, B, WINDOW, SWA_KV_HEADS, SWA_HD),
            from_t(ks), from_t(vs))
```

```python
import functools

import numpy as np
import jax
import jax.numpy as jnp
from jax import lax
from jax.experimental import pallas as pl
from jax.experimental.pallas import tpu as pltpu

F32 = jnp.float32
BF16 = jnp.bfloat16

D_MODEL = 1024
RET_HEADS = 4
RET_DK = 128
RET_DV = 256
RET_QK = RET_HEADS * RET_DK
RET_V = RET_HEADS * RET_DV
SWA_HEADS = 16
SWA_KV_HEADS = 4
SWA_HD = 64
SWA_Q = SWA_HEADS * SWA_HD
SWA_KV = SWA_KV_HEADS * SWA_HD
WINDOW = 128
ROPE_THETA = 10000.0
EPS = 1e-6
PAST_LEN = 8192

C_RQ = 0
C_RK = C_RQ + RET_QK
C_RV = C_RK + RET_QK
C_RG = C_RV + RET_V
C_SQ = C_RG + RET_V
C_SK = C_SQ + SWA_Q
C_SV = C_SK + SWA_KV
C_SG = C_SV + SWA_KV
C_MR = C_SG + SWA_Q
C_MS = C_MR + D_MODEL
IN_WIDTH = C_MS + D_MODEL

LANES = 128
SUBLANES = 8
PROMPT_TM = 256
WCHUNK = 512
VMEM_LIMIT = 60 * 1024 * 1024


def _dot(a, b):
    return jnp.dot(a, b, preferred_element_type=F32)


def _dot_nt(a, b):
    return lax.dot_general(a, b, (((1,), (1,)), ((), ())), preferred_element_type=F32)


def _dot_tn(a, b):
    return lax.dot_general(a, b, (((0,), (0,)), ((), ())), preferred_element_type=F32)


def _sigmoid(x):
    return 1.0 / (1.0 + jnp.exp(-x))


def _rms_rows(x, g):
    return x * lax.rsqrt(jnp.mean(x * x, axis=-1, keepdims=True) + EPS) * g


def _group_mean_sq(x, gmat):
    x2 = x * x
    hi = x2.astype(BF16)
    lo = (x2 - hi.astype(F32)).astype(BF16)
    return _dot(hi, gmat) + _dot(lo, gmat)


def _rope128(x, cos, sin_signed):
    return x * cos + pltpu.roll(x, 64, 1) * sin_signed


def _rope64(x, cos, sin_signed, first_half):
    rot = jnp.where(first_half, pltpu.roll(x, 96, 1), pltpu.roll(x, 32, 1))
    return x * cos + rot * sin_signed


def _swa_norm_rope(x, gmat, gain, cos, sin_signed, first_half):
    xn = x * lax.rsqrt(_group_mean_sq(x, gmat) + EPS) * gain
    cols = [_rope64(xn[:, c * LANES:(c + 1) * LANES], cos, sin_signed, first_half) for c in range(2)]
    return jnp.concatenate(cols, axis=1)


def _heads_to_rows(qrow):
    row = lax.broadcasted_iota(jnp.int32, (SUBLANES, LANES), 0)
    half = lax.broadcasted_iota(jnp.int32, (SUBLANES, LANES), 1) // SWA_HD
    zeros = jnp.zeros((SUBLANES, LANES), F32)
    tiles = []
    for jt in range(2):
        acc = zeros
        for r in range(8):
            sc = 4 * jt + r // 2
            piece = jnp.broadcast_to(qrow[:, sc * LANES:(sc + 1) * LANES], (SUBLANES, LANES))
            if r % 2 != r // 4:
                piece = pltpu.roll(piece, SWA_HD, 1)
            acc = jnp.where((row == r) & (half == r // 4), piece, acc)
        tiles.append(acc)
    return jnp.concatenate([jnp.concatenate([tiles[0], zeros], axis=1),
                            jnp.concatenate([zeros, tiles[1]], axis=1)], axis=0)


def _rows_to_heads(o):
    low = lax.broadcasted_iota(jnp.int32, (1, LANES), 1) < SWA_HD
    cols = []
    for tc in range(SWA_HEADS // 2):
        pieces = []
        for e in range(2):
            h = 2 * tc + e
            j = h // 4
            piece = o[h:h + 1, (j // 2) * LANES:(j // 2 + 1) * LANES]
            if j % 2 != e:
                piece = pltpu.roll(piece, SWA_HD, 1)
            pieces.append(piece)
        cols.append(jnp.where(low, pieces[0], pieces[1]))
    return jnp.concatenate(cols, axis=1)


def _load_weights_bf16(srcs, dsts, stage, sem):
    jobs = [(src, dst, c0) for src, dst in zip(srcs, dsts) for c0 in range(0, src.shape[1], WCHUNK)]

    def copy(i):
        src, _, c0 = jobs[i]
        return pltpu.make_async_copy(src.at[:, pl.ds(c0, WCHUNK)], stage.at[i % 2], sem.at[i % 2])

    copy(0).start()
    for i, (_, dst, c0) in enumerate(jobs):
        if i + 1 < len(jobs):
            copy(i + 1).start()
        copy(i).wait()
        dst[:, c0:c0 + WCHUNK] = stage[i % 2].astype(BF16)


def _layer_kernel(x_ref, ng_ref, win_hbm, rng_ref, qg_ref, kg_ref, sink_ref, wbr_hbm, wbs_hbm, wo_hbm,
                  rcos_ref, rsin_ref, scos_ref, ssin_ref, dmask_ref, cross_ref, kdec_ref, gmat_ref,
                  xs_ref, drcos_ref, drsin_ref, dscos_ref, dssin_ref, dst_ref, dkt_ref, dvt_ref,
                  y_ref, st_ref, ko_ref, vo_ref, ys_ref, dsto_ref, dkto_ref, dvto_ref,
                  s_scr, pk_scr, pv_scr, or_scr, os_scr,
                  drq_scr, drk_scr, drv_scr, dsq_scr, dsk_scr, dsv_scr, dor_scr, dos_scr,
                  win_ref, wbr_ref, wbs_ref, wo_ref, wstage, wsem,
                  *, decay_chunk, decay_step):
    t = pl.program_id(1)
    step = pl.program_id(0) * pl.num_programs(1) + t
    last_step = pl.num_programs(0) * pl.num_programs(1) - 1
    tm = x_ref.shape[0]
    nblk = tm // WINDOW
    nseq = dst_ref.shape[0]

    gmat = gmat_ref[...]
    lane = lax.broadcasted_iota(jnp.int32, (1, LANES), 1)
    first_half = (lane % SWA_HD) < (SWA_HD // 2)
    low = lane < SWA_HD

    @pl.when(step == 0)
    def _():
        _load_weights_bf16((win_hbm, wbr_hbm, wbs_hbm, wo_hbm), (win_ref, wbr_ref, wbs_ref, wo_ref), wstage, wsem)

    def dproj(lo, hi):
        hs = _rms_rows(xs_ref[...], ng_ref[...]).astype(BF16)
        return _dot(hs, win_ref[:, lo:hi])

    @pl.when(step == 0)
    def _():
        drcos = drcos_ref[...]
        drsin = drsin_ref[...]
        rq = dproj(C_RQ, C_RK)
        rk = dproj(C_RK, C_RV)
        for hd in range(RET_HEADS):
            sl = slice(hd * RET_DK, (hd + 1) * RET_DK)
            drq_scr[:, sl] = _rope128(rq[:, sl], drcos, drsin)
            drk_scr[:, sl] = _rope128(rk[:, sl], drcos, drsin) * (RET_DK ** -0.5)
        drv_scr[...] = dproj(C_RV, C_RG)
        dscos = dscos_ref[...]
        dssin = dssin_ref[...]
        for cb in range(SWA_Q // 256):
            dsq_scr[:, cb * 256:(cb + 1) * 256] = _swa_norm_rope(
                dproj(C_SQ + cb * 256, C_SQ + (cb + 1) * 256), gmat, qg_ref[...], dscos, dssin,
                first_half) * (SWA_HD ** -0.5)
        dsk_scr[...] = _swa_norm_rope(dproj(C_SK, C_SV), gmat, kg_ref[...], dscos, dssin, first_half)
        dsv_scr[...] = dproj(C_SV, C_SG)

    @pl.when(t == 0)
    def _():
        s_scr[...] = jnp.zeros_like(s_scr)
        pk_scr[...] = jnp.zeros_like(pk_scr)
        pv_scr[...] = jnp.zeros_like(pv_scr)

    seq0 = step * nseq
    grp = pl.multiple_of((seq0 // SUBLANES) * SUBLANES, SUBLANES)
    rows8 = pl.ds(grp, SUBLANES)
    row8 = lax.broadcasted_iota(jnp.int32, (SUBLANES, 1), 0)
    newest = lane == WINDOW - 1
    drop_oldest = WINDOW - 1 + jnp.minimum(step, 0)
    dstate = {}

    def column_broadcast(rows_scr):
        rowi = lax.broadcasted_iota(jnp.int32, (SUBLANES, SWA_KV), 0)
        ones_row = lax.broadcasted_iota(jnp.int32, (SUBLANES, nseq * LANES), 0)
        ones_lane = lax.broadcasted_iota(jnp.int32, (SUBLANES, nseq * LANES), 1)
        tile = jnp.zeros((SUBLANES, SWA_KV), F32)
        for s in range(nseq):
            r = rows_scr[pl.ds(seq0 + s, 1), :]
            hi = r.astype(BF16).astype(F32)
            mid = (r - hi).astype(BF16).astype(F32)
            lo = (r - hi) - mid
            for i, term in enumerate((hi, mid, lo)):
                tile = jnp.where(rowi == 3 * s + i, jnp.broadcast_to(term, (SUBLANES, SWA_KV)), tile)
        pick = (ones_row // 3 == ones_lane // LANES) & (ones_row < 3 * nseq)
        return _dot_tn(tile.astype(BF16), jnp.where(pick, 1.0, 0.0).astype(BF16))

    def decode_scores():
        q8 = drq_scr[rows8, :]
        k8 = drk_scr[rows8, :]
        v8 = drv_scr[rows8, :]
        mine = (row8 >= seq0 - grp) & (row8 < seq0 - grp + nseq)
        o_acc = [jnp.zeros((SUBLANES, RET_DV), F32)] * RET_HEADS
        knew = column_broadcast(dsk_scr)
        vnew = column_broadcast(dsv_scr)
        scs, vts = [], []
        for s in range(nseq):
            b = seq0 + s
            sel = row8 == (b - grp)
            for hd in range(RET_HEADS):
                qh = q8[:, hd * RET_DK:(hd + 1) * RET_DK]
                kh = k8[:, hd * RET_DK:(hd + 1) * RET_DK]
                vh = v8[:, hd * RET_DV:(hd + 1) * RET_DV]
                state = dst_ref[s, hd]
                inter = _dot(qh.astype(BF16), state.astype(BF16))
                outer = _dot_tn(jnp.where(sel, kh, 0.0).astype(BF16), vh.astype(BF16))
                dsto_ref[s, hd] = decay_step[hd] * state + outer
                o_acc[hd] = jnp.where(sel, inter, o_acc[hd])
            kt_new = jnp.where(newest, knew[:, s * LANES:(s + 1) * LANES], pltpu.roll(dkt_ref[s], drop_oldest, 1))
            vt_new = jnp.where(newest, vnew[:, s * LANES:(s + 1) * LANES], pltpu.roll(dvt_ref[s], drop_oldest, 1))
            dkto_ref[s] = kt_new
            dvto_ref[s] = vt_new
            qexp = _heads_to_rows(dsq_scr[pl.ds(b, 1), :]).astype(BF16)
            scs.append(_dot(qexp, kt_new.astype(BF16)))
            vts.append(vt_new.astype(BF16))
        for hd in range(RET_HEADS):
            qh = q8[:, hd * RET_DK:(hd + 1) * RET_DK]
            kh = k8[:, hd * RET_DK:(hd + 1) * RET_DK]
            vh = v8[:, hd * RET_DV:(hd + 1) * RET_DV]
            sl = slice(hd * RET_DV, (hd + 1) * RET_DV)
            o = jnp.sum(qh * kh, axis=-1, keepdims=True) * vh + decay_step[hd] * o_acc[hd]
            dor_scr[rows8, sl] = jnp.where(mine, o, dor_scr[rows8, sl])
        dstate["scs"], dstate["vts"] = scs, vts

    def decode_attend():
        head = lax.broadcasted_iota(jnp.int32, (SWA_HEADS, 1), 0)
        sink_col = jnp.zeros((SWA_HEADS, 1), F32)
        for hh in range(SWA_HEADS):
            sink_col = jnp.where(head == hh, sink_ref[hh], sink_col)
        outs = []
        for s in range(nseq):
            sc = dstate["scs"][s]
            m = jnp.maximum(jnp.max(sc, axis=-1, keepdims=True), sink_col)
            e = jnp.exp(sc - m)
            p = e / (jnp.sum(e, axis=-1, keepdims=True) + jnp.exp(sink_col - m))
            pt = jnp.concatenate([p.astype(BF16), jnp.zeros((LANES - SWA_HEADS, WINDOW), BF16)], axis=0)
            out_t = _dot_nt(dstate["vts"][s], pt)
            outs.append(out_t.T[:SWA_HEADS])
        dstate["outs"] = outs

    def decode_store():
        for s in range(nseq):
            dos_scr[pl.ds(seq0 + s, 1), :] = _rows_to_heads(dstate["outs"][s])

    x = x_ref[...]
    h = _rms_rows(x, ng_ref[...]).astype(BF16)

    def proj(lo, hi):
        return _dot(h, win_ref[:, lo:hi])

    scos = scos_ref[...]
    ssin = ssin_ref[...]

    rq = proj(C_RQ, C_RK)
    decode_scores()
    rk = proj(C_RK, C_RV)
    rv = proj(C_RV, C_RG).astype(BF16)
    decode_attend()
    rcos = rcos_ref[...]
    rsin = rsin_ref[...]
    k_scale = RET_DK ** -0.5
    scores, inter, vs = [], [], []
    for hd in range(RET_HEADS):
        q = _rope128(rq[:, hd * RET_DK:(hd + 1) * RET_DK], rcos, rsin)
        k = _rope128(rk[:, hd * RET_DK:(hd + 1) * RET_DK], rcos, rsin) * k_scale
        v = rv[:, hd * RET_DV:(hd + 1) * RET_DV]
        state = s_scr[hd]
        scores.append(_dot_nt(q.astype(BF16), k.astype(BF16)))
        inter.append(_dot((q * cross_ref[hd]).astype(BF16), state.astype(BF16)))
        s_scr[hd] = decay_chunk[hd] * state + _dot_tn((k * kdec_ref[hd]).astype(BF16), v)
        vs.append(v)
    decode_store()
    sq_cols = []
    sk = sv = None
    for hd in range(RET_HEADS):
        g = proj(C_RG + hd * RET_DV, C_RG + (hd + 1) * RET_DV)
        sq_cols.append(_swa_norm_rope(proj(C_SQ + hd * 256, C_SQ + (hd + 1) * 256), gmat, qg_ref[...], scos, ssin,
                                      first_half) * (SWA_HD ** -0.5))
        if hd == 0:
            sk = _swa_norm_rope(proj(C_SK, C_SV), gmat, kg_ref[...], scos, ssin, first_half)
        if hd == 1:
            sv = proj(C_SV, C_SG)
        o = _dot((scores[hd] * dmask_ref[hd]).astype(BF16), vs[hd]) + inter[hd]
        on = _rms_rows(o, rng_ref[hd:hd + 1, :])
        or_scr[:, hd * RET_DV:(hd + 1) * RET_DV] = (on * (g * _sigmoid(g))).astype(BF16)

    row_i = lax.broadcasted_iota(jnp.int32, (WINDOW, 2 * WINDOW), 0)
    col_j = lax.broadcasted_iota(jnp.int32, (WINDOW, 2 * WINDOW), 1)
    allowed, kbds, vbds = [], {}, {}
    for c in range(nblk):
        lower = row_i + 1
        if c == 0:
            lower = jnp.maximum(lower, jnp.where(t == 0, WINDOW, 0))
        allowed.append((col_j >= lower) & (col_j <= row_i + WINDOW))

    def block_diag(c, j, src, prev_scr):
        r0 = c * WINDOW
        prev = prev_scr[...] if c == 0 else src[r0 - WINDOW:r0]
        cat = jnp.concatenate([prev, src[r0:r0 + WINDOW]], axis=0)
        col, half = divmod(j, 2)
        a = cat[:, col * LANES:(col + 1) * LANES]
        ar = pltpu.roll(a, SWA_HD, 1)
        if half == 0:
            bd = jnp.concatenate([jnp.where(low, a, 0.0), jnp.where(low, 0.0, ar)], axis=0)
        else:
            bd = jnp.concatenate([jnp.where(low, ar, 0.0), jnp.where(low, 0.0, a)], axis=0)
        return bd.astype(BF16)

    pairs = [(c, p) for c in range(nblk) for p in range(SWA_HEADS // 2)]

    def qk(i):
        c, p = pairs[i]
        j = p // 2
        if (c, j) not in kbds:
            kbds[(c, j)] = block_diag(c, j, sk, pk_scr)
            vbds[(c, j)] = block_diag(c, j, sv, pv_scr)
        qp = sq_cols[p // 2][c * WINDOW:(c + 1) * WINDOW, (p % 2) * LANES:(p % 2 + 1) * LANES].astype(BF16)
        return _dot_nt(qp, kbds[(c, j)])

    chunks = {}

    def filler(i):
        kind, cb = divmod(i, 4)
        if kind == 0:
            chunks["sg", cb] = proj(C_SG + cb * 256, C_SG + (cb + 1) * 256)
        elif kind == 1:
            chunks["mr", cb] = _sigmoid(proj(C_MR + cb * 256, C_MR + (cb + 1) * 256))
        elif kind == 2:
            chunks["ms", cb] = _sigmoid(proj(C_MS + cb * 256, C_MS + (cb + 1) * 256))
        else:
            chunks["br", cb] = _dot(or_scr[...], wbr_ref[:, cb * 256:(cb + 1) * 256])

    assert nblk == 2
    pairs.sort(key=lambda cp: (cp[1], cp[0]))
    order = [0, 4, 8, 12, 1, 5, 9, 13, 2, 6, 10, 14, 3, 7, 11, 15]
    s2_next = qk(0)
    for i, (c, p) in enumerate(pairs):
        s2 = s2_next
        if i + 1 < len(pairs):
            s2_next = qk(i + 1)
        filler(order[i])
        es, invs = [], []
        for a in range(2):
            sink = sink_ref[2 * p + a]
            s = jnp.where(allowed[c], s2[:, a * 2 * WINDOW:(a + 1) * 2 * WINDOW], -jnp.inf)
            m = jnp.maximum(jnp.max(s, axis=-1, keepdims=True), sink)
            e = jnp.exp(s - m)
            invs.append(1.0 / (jnp.sum(e, axis=-1, keepdims=True) + jnp.exp(sink - m)))
            es.append(e.astype(BF16))
        o2 = _dot(jnp.concatenate(es, axis=1), vbds[(c, p // 2)])
        o2 = o2 * jnp.where(low, invs[0], invs[1])
        g = chunks["sg", p // 2][c * WINDOW:(c + 1) * WINDOW, (p % 2) * LANES:(p % 2 + 1) * LANES]
        os_scr[c * WINDOW:(c + 1) * WINDOW, p * LANES:(p + 1) * LANES] = (o2 * (g * _sigmoid(g))).astype(BF16)
    pk_scr[...] = sk[tm - WINDOW:]
    pv_scr[...] = sv[tm - WINDOW:]

    merged = []
    for cb in range(D_MODEL // 256):
        br_s = _dot(os_scr[...], wbs_ref[:, cb * 256:(cb + 1) * 256])
        merged.append((chunks["mr", cb] * chunks["br", cb] + chunks["ms", cb] * br_s).astype(BF16))
    y_ref[...] = x + _dot(jnp.concatenate(merged, axis=1), wo_ref[...])

    @pl.when(t == pl.num_programs(1) - 1)
    def _():
        st_ref[...] = s_scr[...]
        ko_ref[...] = sk[tm - WINDOW:].T
        vo_ref[...] = sv[tm - WINDOW:].T

    @pl.when(step == last_step)
    def _():
        rg = dproj(C_RG, C_SQ)
        o_r = dor_scr[...]
        parts = []
        for hd in range(RET_HEADS):
            sl = slice(hd * RET_DV, (hd + 1) * RET_DV)
            parts.append(_rms_rows(o_r[:, sl], rng_ref[hd:hd + 1, :]) * (rg[:, sl] * _sigmoid(rg[:, sl])))
        br_r = _dot(jnp.concatenate(parts, axis=1).astype(BF16), wbr_ref[...])
        sg = dproj(C_SG, C_MR)
        br_s = _dot((dos_scr[...] * (sg * _sigmoid(sg))).astype(BF16), wbs_ref[...])
        mrg = _sigmoid(dproj(C_MR, C_MS)) * br_r + _sigmoid(dproj(C_MS, IN_WIDTH)) * br_s
        ys_ref[...] = xs_ref[...] + _dot(mrg.astype(BF16), wo_ref[...])


def _const_spec(shape):
    nd = len(shape)
    return pl.BlockSpec(shape, lambda *_: (0,) * nd, pipeline_mode=pl.Buffered(1))


def _rope_tables(pos, d):
    inv = np.float64(ROPE_THETA) ** (-np.arange(0, d, 2, dtype=np.float64) / d)
    ang = pos.astype(np.float64)[:, None] * inv[None, :]
    c, s = np.cos(ang), np.sin(ang)
    reps = LANES // d
    cos = np.tile(np.concatenate([c, c], axis=1), (1, reps))
    sin = np.tile(np.concatenate([-s, s], axis=1), (1, reps))
    return cos.astype(np.float32), sin.astype(np.float32)


def _log_decay():
    return np.log(1.0 - 2.0 ** (-5.0 - np.arange(RET_HEADS, dtype=np.float64)))


def _group_matrix():
    g = np.arange(256) // SWA_HD
    return jnp.asarray((g[:, None] == g[None, :]).astype(np.float32) / SWA_HD, dtype=BF16)


def _fused_layer(x, xs, state, cache_kt, cache_vt, w, tm):
    B, T, D = x.shape
    nt = T // tm
    nsteps = B * nt
    nb = xs.shape[0]
    assert nb % nsteps == 0 and nb % SUBLANES == 0
    nseq = nb // nsteps
    assert SUBLANES % nseq == 0 and 3 * nseq <= SUBLANES

    lg = _log_decay()
    n = np.arange(tm, dtype=np.float64)
    diff = n[:, None] - n[None, :]
    dmask = np.where(diff[None] >= 0, np.exp(np.maximum(diff, 0.0)[None] * lg[:, None, None]), 0.0).astype(np.float32)
    cross = np.broadcast_to(np.exp((n[None, :] + 1.0) * lg[:, None])[:, :, None], (RET_HEADS, tm, RET_DK))
    kdec = np.broadcast_to(np.exp((tm - 1.0 - n)[None, :] * lg[:, None])[:, :, None], (RET_HEADS, tm, RET_DK))
    decay_chunk = tuple(float(v) for v in np.exp(tm * lg))
    decay_step = tuple(float(v) for v in np.exp(lg))
    pos = np.arange(T)
    rcos, rsin = _rope_tables(pos, RET_DK)
    scos, ssin = _rope_tables(pos, SWA_HD)
    dpos = np.full((1,), PAST_LEN)
    drcos, drsin = _rope_tables(dpos, RET_DK)
    dscos, dssin = _rope_tables(dpos, SWA_HD)

    tok = lambda b, t: (b, t, 0)
    tab = lambda b, t: (t, 0)
    seq4 = lambda b, t: (b * nt + t, 0, 0, 0)
    seq3 = lambda b, t: (b * nt + t, 0, 0)
    in_specs = [
        pl.BlockSpec((None, tm, D), tok),
        _const_spec((1, D)),
        pl.BlockSpec(memory_space=pltpu.HBM),
        _const_spec((RET_HEADS, RET_DV)),
        _const_spec((1, 256)),
        _const_spec((1, 256)),
        pl.BlockSpec(memory_space=pltpu.SMEM),
        pl.BlockSpec(memory_space=pltpu.HBM),
        pl.BlockSpec(memory_space=pltpu.HBM),
        pl.BlockSpec(memory_space=pltpu.HBM),
        pl.BlockSpec((tm, LANES), tab),
        pl.BlockSpec((tm, LANES), tab),
        pl.BlockSpec((tm, LANES), tab),
        pl.BlockSpec((tm, LANES), tab),
        _const_spec((RET_HEADS, tm, tm)),
        _const_spec((RET_HEADS, tm, RET_DK)),
        _const_spec((RET_HEADS, tm, RET_DK)),
        _const_spec((256, 256)),
        _const_spec((nb, D)),
        _const_spec((1, LANES)),
        _const_spec((1, LANES)),
        _const_spec((1, LANES)),
        _const_spec((1, LANES)),
        pl.BlockSpec((nseq, RET_HEADS, RET_DK, RET_DV), seq4),
        pl.BlockSpec((nseq, SWA_KV, WINDOW), seq3),
        pl.BlockSpec((nseq, SWA_KV, WINDOW), seq3),
    ]
    out_shape = (
        jax.ShapeDtypeStruct((B, T, D), F32),
        jax.ShapeDtypeStruct((B, RET_HEADS, RET_DK, RET_DV), F32),
        jax.ShapeDtypeStruct((B, SWA_KV, WINDOW), F32),
        jax.ShapeDtypeStruct((B, SWA_KV, WINDOW), F32),
        jax.ShapeDtypeStruct((nb, D), F32),
        jax.ShapeDtypeStruct(state.shape, F32),
        jax.ShapeDtypeStruct(cache_kt.shape, F32),
        jax.ShapeDtypeStruct(cache_vt.shape, F32),
    )
    out_specs = (
        pl.BlockSpec((None, tm, D), tok),
        pl.BlockSpec((None, RET_HEADS, RET_DK, RET_DV), lambda b, t: (b, 0, 0, 0)),
        pl.BlockSpec((None, SWA_KV, WINDOW), lambda b, t: (b, 0, 0)),
        pl.BlockSpec((None, SWA_KV, WINDOW), lambda b, t: (b, 0, 0)),
        pl.BlockSpec((nb, D), lambda b, t: (0, 0)),
        pl.BlockSpec((nseq, RET_HEADS, RET_DK, RET_DV), seq4),
        pl.BlockSpec((nseq, SWA_KV, WINDOW), seq3),
        pl.BlockSpec((nseq, SWA_KV, WINDOW), seq3),
    )
    scratch = [
        pltpu.VMEM((RET_HEADS, RET_DK, RET_DV), F32),
        pltpu.VMEM((WINDOW, SWA_KV), F32),
        pltpu.VMEM((WINDOW, SWA_KV), F32),
        pltpu.VMEM((tm, RET_V), BF16),
        pltpu.VMEM((tm, SWA_Q), BF16),
        pltpu.VMEM((nb, RET_QK), F32),
        pltpu.VMEM((nb, RET_QK), F32),
        pltpu.VMEM((nb, RET_V), F32),
        pltpu.VMEM((nb, SWA_Q), F32),
        pltpu.VMEM((nb, SWA_KV), F32),
        pltpu.VMEM((nb, SWA_KV), F32),
        pltpu.VMEM((nb, RET_V), F32),
        pltpu.VMEM((nb, SWA_Q), F32),
        pltpu.VMEM((D, IN_WIDTH), BF16),
        pltpu.VMEM((RET_V, D), BF16),
        pltpu.VMEM((SWA_Q, D), BF16),
        pltpu.VMEM((D, D), BF16),
        pltpu.VMEM((2, D, WCHUNK), F32),
        pltpu.SemaphoreType.DMA((2,)),
    ]
    return pl.pallas_call(
        functools.partial(_layer_kernel, decay_chunk=decay_chunk, decay_step=decay_step),
        grid=(B, nt),
        in_specs=in_specs,
        out_specs=out_specs,
        out_shape=out_shape,
        scratch_shapes=scratch,
        compiler_params=pltpu.CompilerParams(
            dimension_semantics=("arbitrary", "arbitrary"), vmem_limit_bytes=VMEM_LIMIT),
        name="hybrid_layer",
    )(x, w["ng"], w["win"], w["rng"], w["qg"], w["kg"], w["sinks"], w["wbr"], w["wbs"], w["wo"],
      rcos, rsin, scos, ssin, dmask, cross.astype(np.float32), kdec.astype(np.float32), _group_matrix(),
      xs, drcos, drsin, dscos, dssin, state, cache_kt, cache_vt)


def kernel(x_prompt, x_sample, state_ret, cache_swa_k, cache_swa_v, norm_g, w_in, ret_norm_g, swa_q_g, swa_k_g,
           swa_sinks, w_br_ret, w_br_swa, w_out):
    depth = norm_g.shape[0]
    assert depth == 1 and x_sample.shape[1] == 1
    B, T, D = x_prompt.shape
    nb = x_sample.shape[0]
    wc = cache_swa_k.shape[2]
    assert wc == WINDOW and nb == LANES and T % PROMPT_TM == 0
    l = 0
    w = {
        "ng": norm_g[l][None, :],
        "win": w_in[l],
        "rng": ret_norm_g[l],
        "qg": jnp.tile(swa_q_g[l], 256 // SWA_HD)[None, :],
        "kg": jnp.tile(swa_k_g[l], 256 // SWA_HD)[None, :],
        "sinks": swa_sinks[l],
        "wbr": w_br_ret[l],
        "wbs": w_br_swa[l],
        "wo": w_out[l],
    }
    to_t = lambda c: jnp.transpose(c.reshape(nb, wc, SWA_KV), (0, 2, 1))
    from_t = lambda c: jnp.transpose(c, (0, 2, 1)).reshape(1, c.shape[0], wc, SWA_KV_HEADS, SWA_HD)
    yp, rp, kp, vp, ys, rs, ks, vs = _fused_layer(
        x_prompt, x_sample[:, 0, :], state_ret[l], to_t(cache_swa_k[l]), to_t(cache_swa_v[l]), w, PROMPT_TM)
    return (yp, ys[:, None, :], rp[None], rs[None],
            from_t(kp), from_t(vp), from_t(ks), from_t(vs))
```

```python
import functools

import numpy as np
import jax
import jax.numpy as jnp
from jax import lax
from jax.experimental import pallas as pl
from jax.experimental.pallas import tpu as pltpu

F32 = jnp.float32
BF16 = jnp.bfloat16

D_MODEL = 1024
RET_HEADS = 4
RET_DK = 128
RET_DV = 256
RET_QK = RET_HEADS * RET_DK
RET_V = RET_HEADS * RET_DV
SWA_HEADS = 16
SWA_KV_HEADS = 4
SWA_HD = 64
SWA_Q = SWA_HEADS * SWA_HD
SWA_KV = SWA_KV_HEADS * SWA_HD
WINDOW = 128
ROPE_THETA = 10000.0
EPS = 1e-6
PAST_LEN = 8192

C_RQ = 0
C_RK = C_RQ + RET_QK
C_RV = C_RK + RET_QK
C_RG = C_RV + RET_V
C_SQ = C_RG + RET_V
C_SK = C_SQ + SWA_Q
C_SV = C_SK + SWA_KV
C_SG = C_SV + SWA_KV
C_MR = C_SG + SWA_Q
C_MS = C_MR + D_MODEL
IN_WIDTH = C_MS + D_MODEL

LANES = 128
SUBLANES = 8
PROMPT_TM = 256
WCHUNK = 512
VMEM_LIMIT = 60 * 1024 * 1024


def _dot(a, b):
    return jnp.dot(a, b, preferred_element_type=F32)


def _dot_nt(a, b):
    return lax.dot_general(a, b, (((1,), (1,)), ((), ())), preferred_element_type=F32)


def _dot_tn(a, b):
    return lax.dot_general(a, b, (((0,), (0,)), ((), ())), preferred_element_type=F32)


def _sigmoid(x):
    return 1.0 / (1.0 + jnp.exp(-x))


def _rms_rows(x, g):
    return x * lax.rsqrt(jnp.mean(x * x, axis=-1, keepdims=True) + EPS) * g


def _group_mean_sq(x, gmat):
    x2 = x * x
    hi = x2.astype(BF16)
    lo = (x2 - hi.astype(F32)).astype(BF16)
    return _dot(hi, gmat) + _dot(lo, gmat)


def _rope128(x, cos, sin_signed):
    return x * cos + pltpu.roll(x, 64, 1) * sin_signed


def _rope64(x, cos, sin_signed, first_half):
    rot = jnp.where(first_half, pltpu.roll(x, 96, 1), pltpu.roll(x, 32, 1))
    return x * cos + rot * sin_signed


def _swa_norm_rope(x, gmat, gain, cos, sin_signed, first_half):
    xn = x * lax.rsqrt(_group_mean_sq(x, gmat) + EPS) * gain
    cols = [_rope64(xn[:, c * LANES:(c + 1) * LANES], cos, sin_signed, first_half) for c in range(2)]
    return jnp.concatenate(cols, axis=1)


def _heads_to_rows(qrow):
    row = lax.broadcasted_iota(jnp.int32, (SUBLANES, LANES), 0)
    half = lax.broadcasted_iota(jnp.int32, (SUBLANES, LANES), 1) // SWA_HD
    zeros = jnp.zeros((SUBLANES, LANES), F32)
    tiles = []
    for jt in range(2):
        acc = zeros
        for r in range(8):
            sc = 4 * jt + r // 2
            piece = jnp.broadcast_to(qrow[:, sc * LANES:(sc + 1) * LANES], (SUBLANES, LANES))
            if r % 2 != r // 4:
                piece = pltpu.roll(piece, SWA_HD, 1)
            acc = jnp.where((row == r) & (half == r // 4), piece, acc)
        tiles.append(acc)
    return jnp.concatenate([jnp.concatenate([tiles[0], zeros], axis=1),
                            jnp.concatenate([zeros, tiles[1]], axis=1)], axis=0)


def _rows_to_heads(o):
    low = lax.broadcasted_iota(jnp.int32, (1, LANES), 1) < SWA_HD
    cols = []
    for tc in range(SWA_HEADS // 2):
        pieces = []
        for e in range(2):
            h = 2 * tc + e
            j = h // 4
            piece = o[h:h + 1, (j // 2) * LANES:(j // 2 + 1) * LANES]
            if j % 2 != e:
                piece = pltpu.roll(piece, SWA_HD, 1)
            pieces.append(piece)
        cols.append(jnp.where(low, pieces[0], pieces[1]))
    return jnp.concatenate(cols, axis=1)


def _load_weights_bf16(srcs, dsts, stage, sem):
    jobs = [(src, dst, c0) for src, dst in zip(srcs, dsts) for c0 in range(0, src.shape[1], WCHUNK)]

    def copy(i):
        src, _, c0 = jobs[i]
        return pltpu.make_async_copy(src.at[:, pl.ds(c0, WCHUNK)], stage.at[i % 2], sem.at[i % 2])

    copy(0).start()
    for i, (_, dst, c0) in enumerate(jobs):
        if i + 1 < len(jobs):
            copy(i + 1).start()
        copy(i).wait()
        dst[:, c0:c0 + WCHUNK] = stage[i % 2].astype(BF16)


def _layer_kernel(x_ref, ng_ref, win_hbm, rng_ref, qg_ref, kg_ref, sink_ref, wbr_hbm, wbs_hbm, wo_hbm,
                  rcos_ref, rsin_ref, scos_ref, ssin_ref, dmask_ref, cross_ref, kdec_ref, gmat_ref,
                  xs_ref, drcos_ref, drsin_ref, dscos_ref, dssin_ref, dst_ref, dkt_ref, dvt_ref,
                  y_ref, st_ref, ko_ref, vo_ref, ys_ref, dsto_ref, dkto_ref, dvto_ref,
                  s_scr, pk_scr, pv_scr, or_scr, os_scr,
                  drq_scr, drk_scr, drv_scr, dsq_scr, dsk_scr, dsv_scr, dor_scr, dos_scr,
                  win_ref, wbr_ref, wbs_ref, wo_ref, wstage, wsem,
                  *, decay_chunk, decay_step):
    t = pl.program_id(1)
    step = pl.program_id(0) * pl.num_programs(1) + t
    last_step = pl.num_programs(0) * pl.num_programs(1) - 1
    tm = x_ref.shape[0]
    nblk = tm // WINDOW
    nseq = dst_ref.shape[0]

    gmat = gmat_ref[...]
    lane = lax.broadcasted_iota(jnp.int32, (1, LANES), 1)
    first_half = (lane % SWA_HD) < (SWA_HD // 2)
    low = lane < SWA_HD

    @pl.when(step == 0)
    def _():
        _load_weights_bf16((win_hbm, wbr_hbm, wbs_hbm, wo_hbm), (win_ref, wbr_ref, wbs_ref, wo_ref), wstage, wsem)

    def dproj(lo, hi):
        hs = _rms_rows(xs_ref[...], ng_ref[...]).astype(BF16)
        return _dot(hs, win_ref[:, lo:hi])

    @pl.when(step == 0)
    def _():
        drcos = drcos_ref[...]
        drsin = drsin_ref[...]
        rq = dproj(C_RQ, C_RK)
        rk = dproj(C_RK, C_RV)
        for hd in range(RET_HEADS):
            sl = slice(hd * RET_DK, (hd + 1) * RET_DK)
            drq_scr[:, sl] = _rope128(rq[:, sl], drcos, drsin)
            drk_scr[:, sl] = _rope128(rk[:, sl], drcos, drsin) * (RET_DK ** -0.5)
        drv_scr[...] = dproj(C_RV, C_RG)
        dscos = dscos_ref[...]
        dssin = dssin_ref[...]
        for cb in range(SWA_Q // 256):
            dsq_scr[:, cb * 256:(cb + 1) * 256] = _swa_norm_rope(
                dproj(C_SQ + cb * 256, C_SQ + (cb + 1) * 256), gmat, qg_ref[...], dscos, dssin,
                first_half) * (SWA_HD ** -0.5)
        dsk_scr[...] = _swa_norm_rope(dproj(C_SK, C_SV), gmat, kg_ref[...], dscos, dssin, first_half)
        dsv_scr[...] = dproj(C_SV, C_SG)

    @pl.when(t == 0)
    def _():
        s_scr[...] = jnp.zeros_like(s_scr)
        pk_scr[...] = jnp.zeros_like(pk_scr)
        pv_scr[...] = jnp.zeros_like(pv_scr)

    seq0 = step * nseq
    grp = pl.multiple_of((seq0 // SUBLANES) * SUBLANES, SUBLANES)
    rows8 = pl.ds(grp, SUBLANES)
    row8 = lax.broadcasted_iota(jnp.int32, (SUBLANES, 1), 0)
    newest = lane == WINDOW - 1
    drop_oldest = WINDOW - 1 + jnp.minimum(step, 0)
    dstate = {}

    def column_broadcast(rows_scr):
        rowi = lax.broadcasted_iota(jnp.int32, (SUBLANES, SWA_KV), 0)
        ones_row = lax.broadcasted_iota(jnp.int32, (SUBLANES, nseq * LANES), 0)
        ones_lane = lax.broadcasted_iota(jnp.int32, (SUBLANES, nseq * LANES), 1)
        tile = jnp.zeros((SUBLANES, SWA_KV), F32)
        for s in range(nseq):
            r = rows_scr[pl.ds(seq0 + s, 1), :]
            hi = r.astype(BF16).astype(F32)
            mid = (r - hi).astype(BF16).astype(F32)
            lo = (r - hi) - mid
            for i, term in enumerate((hi, mid, lo)):
                tile = jnp.where(rowi == 3 * s + i, jnp.broadcast_to(term, (SUBLANES, SWA_KV)), tile)
        pick = (ones_row // 3 == ones_lane // LANES) & (ones_row < 3 * nseq)
        return _dot_tn(tile.astype(BF16), jnp.where(pick, 1.0, 0.0).astype(BF16))

    def decode_scores():
        q8 = drq_scr[rows8, :]
        k8 = drk_scr[rows8, :]
        v8 = drv_scr[rows8, :]
        mine = (row8 >= seq0 - grp) & (row8 < seq0 - grp + nseq)
        o_acc = [jnp.zeros((SUBLANES, RET_DV), F32)] * RET_HEADS
        knew = column_broadcast(dsk_scr)
        vnew = column_broadcast(dsv_scr)
        scs, vts = [], []
        for s in range(nseq):
            b = seq0 + s
            sel = row8 == (b - grp)
            for hd in range(RET_HEADS):
                qh = q8[:, hd * RET_DK:(hd + 1) * RET_DK]
                kh = k8[:, hd * RET_DK:(hd + 1) * RET_DK]
                vh = v8[:, hd * RET_DV:(hd + 1) * RET_DV]
                state = dst_ref[s, hd]
                inter = _dot(qh.astype(BF16), state.astype(BF16))
                outer = _dot_tn(jnp.where(sel, kh, 0.0).astype(BF16), vh.astype(BF16))
                dsto_ref[s, hd] = decay_step[hd] * state + outer
                o_acc[hd] = jnp.where(sel, inter, o_acc[hd])
            kt_new = jnp.where(newest, knew[:, s * LANES:(s + 1) * LANES], pltpu.roll(dkt_ref[s], drop_oldest, 1))
            vt_new = jnp.where(newest, vnew[:, s * LANES:(s + 1) * LANES], pltpu.roll(dvt_ref[s], drop_oldest, 1))
            dkto_ref[s] = kt_new
            dvto_ref[s] = vt_new
            qexp = _heads_to_rows(dsq_scr[pl.ds(b, 1), :]).astype(BF16)
            scs.append(_dot(qexp, kt_new.astype(BF16)))
            vts.append(vt_new.astype(BF16))
        for hd in range(RET_HEADS):
            qh = q8[:, hd * RET_DK:(hd + 1) * RET_DK]
            kh = k8[:, hd * RET_DK:(hd + 1) * RET_DK]
            vh = v8[:, hd * RET_DV:(hd + 1) * RET_DV]
            sl = slice(hd * RET_DV, (hd + 1) * RET_DV)
            o = jnp.sum(qh * kh, axis=-1, keepdims=True) * vh + decay_step[hd] * o_acc[hd]
            dor_scr[rows8, sl] = jnp.where(mine, o, dor_scr[rows8, sl])
        dstate["scs"], dstate["vts"] = scs, vts

    def decode_attend():
        head = lax.broadcasted_iota(jnp.int32, (SWA_HEADS, 1), 0)
        sink_col = jnp.zeros((SWA_HEADS, 1), F32)
        for hh in range(SWA_HEADS):
            sink_col = jnp.where(head == hh, sink_ref[hh], sink_col)
        outs = []
        for s in range(nseq):
            sc = dstate["scs"][s]
            m = jnp.maximum(jnp.max(sc, axis=-1, keepdims=True), sink_col)
            e = jnp.exp(sc - m)
            p = e / (jnp.sum(e, axis=-1, keepdims=True) + jnp.exp(sink_col - m))
            pt = jnp.concatenate([p.astype(BF16), jnp.zeros((LANES - SWA_HEADS, WINDOW), BF16)], axis=0)
            out_t = _dot_nt(dstate["vts"][s], pt)
            outs.append(out_t.T[:SWA_HEADS])
        dstate["outs"] = outs

    def decode_store():
        for s in range(nseq):
            dos_scr[pl.ds(seq0 + s, 1), :] = _rows_to_heads(dstate["outs"][s])

    x = x_ref[...]
    h = _rms_rows(x, ng_ref[...]).astype(BF16)

    def proj(lo, hi):
        return _dot(h, win_ref[:, lo:hi])

    scos = scos_ref[...]
    ssin = ssin_ref[...]

    rq = proj(C_RQ, C_RK)
    decode_scores()
    rk = proj(C_RK, C_RV)
    rv = proj(C_RV, C_RG).astype(BF16)
    decode_attend()
    rcos = rcos_ref[...]
    rsin = rsin_ref[...]
    k_scale = RET_DK ** -0.5
    scores, inter, vs = [], [], []
    for hd in range(RET_HEADS):
        q = _rope128(rq[:, hd * RET_DK:(hd + 1) * RET_DK], rcos, rsin)
        k = _rope128(rk[:, hd * RET_DK:(hd + 1) * RET_DK], rcos, rsin) * k_scale
        v = rv[:, hd * RET_DV:(hd + 1) * RET_DV]
        state = s_scr[hd]
        scores.append(_dot_nt(q.astype(BF16), k.astype(BF16)))
        inter.append(_dot((q * cross_ref[hd]).astype(BF16), state.astype(BF16)))
        s_scr[hd] = decay_chunk[hd] * state + _dot_tn((k * kdec_ref[hd]).astype(BF16), v)
        vs.append(v)
    decode_store()
    sq_cols = []
    sk = sv = None
    for hd in range(RET_HEADS):
        g = proj(C_RG + hd * RET_DV, C_RG + (hd + 1) * RET_DV)
        sq_cols.append(_swa_norm_rope(proj(C_SQ + hd * 256, C_SQ + (hd + 1) * 256), gmat, qg_ref[...], scos, ssin,
                                      first_half) * (SWA_HD ** -0.5))
        if hd == 0:
            sk = _swa_norm_rope(proj(C_SK, C_SV), gmat, kg_ref[...], scos, ssin, first_half)
        if hd == 1:
            sv = proj(C_SV, C_SG)
        o = _dot((scores[hd] * dmask_ref[hd]).astype(BF16), vs[hd]) + inter[hd]
        on = _rms_rows(o, rng_ref[hd:hd + 1, :])
        or_scr[:, hd * RET_DV:(hd + 1) * RET_DV] = (on * (g * _sigmoid(g))).astype(BF16)

    row_i = lax.broadcasted_iota(jnp.int32, (WINDOW, 2 * WINDOW), 0)
    col_j = lax.broadcasted_iota(jnp.int32, (WINDOW, 2 * WINDOW), 1)
    allowed, kbds, vbds = [], {}, {}
    for c in range(nblk):
        lower = row_i + 1
        if c == 0:
            lower = jnp.maximum(lower, jnp.where(t == 0, WINDOW, 0))
        allowed.append((col_j >= lower) & (col_j <= row_i + WINDOW))

    def block_diag(c, j, src, prev_scr):
        r0 = c * WINDOW
        prev = prev_scr[...] if c == 0 else src[r0 - WINDOW:r0]
        cat = jnp.concatenate([prev, src[r0:r0 + WINDOW]], axis=0)
        col, half = divmod(j, 2)
        a = cat[:, col * LANES:(col + 1) * LANES]
        ar = pltpu.roll(a, SWA_HD, 1)
        if half == 0:
            bd = jnp.concatenate([jnp.where(low, a, 0.0), jnp.where(low, 0.0, ar)], axis=0)
        else:
            bd = jnp.concatenate([jnp.where(low, ar, 0.0), jnp.where(low, 0.0, a)], axis=0)
        return bd.astype(BF16)

    quads = [(c, j) for j in range(SWA_KV_HEADS) for c in range(nblk)]

    def qk(i):
        c, j = quads[i]
        kbds[(c, j)] = block_diag(c, j, sk, pk_scr)
        vbds[(c, j)] = block_diag(c, j, sv, pv_scr)
        qp = jnp.concatenate([sq_cols[j][c * WINDOW:(c + 1) * WINDOW, r * LANES:(r + 1) * LANES] for r in range(2)],
                             axis=0).astype(BF16)
        return _dot_nt(qp, kbds[(c, j)])

    chunks = {}

    def filler(i):
        kind, cb = divmod(i, 4)
        if kind == 0:
            chunks["sg", cb] = proj(C_SG + cb * 256, C_SG + (cb + 1) * 256)
        elif kind == 1:
            chunks["mr", cb] = _sigmoid(proj(C_MR + cb * 256, C_MR + (cb + 1) * 256))
        elif kind == 2:
            chunks["ms", cb] = _sigmoid(proj(C_MS + cb * 256, C_MS + (cb + 1) * 256))
        else:
            chunks["br", cb] = _dot(or_scr[...], wbr_ref[:, cb * 256:(cb + 1) * 256])

    order = [0, 4, 8, 12, 1, 5, 9, 13, 2, 6, 10, 14, 3, 7, 11, 15]
    per_unit = len(order) // len(quads)
    s4_next = qk(0)
    for i, (c, j) in enumerate(quads):
        s4 = s4_next
        if i + 1 < len(quads):
            s4_next = qk(i + 1)
        for f in order[i * per_unit:(i + 1) * per_unit]:
            filler(f)
        e_rows, inv_rows = [], []
        for r in range(2):
            es, invs = [], []
            for a in range(2):
                sink = sink_ref[4 * j + 2 * r + a]
                s = jnp.where(allowed[c], s4[r * WINDOW:(r + 1) * WINDOW, a * 2 * WINDOW:(a + 1) * 2 * WINDOW], -jnp.inf)
                m = jnp.maximum(jnp.max(s, axis=-1, keepdims=True), sink)
                e = jnp.exp(s - m)
                invs.append(1.0 / (jnp.sum(e, axis=-1, keepdims=True) + jnp.exp(sink - m)))
                es.append(e.astype(BF16))
            e_rows.append(jnp.concatenate(es, axis=1))
            inv_rows.append(jnp.where(low, invs[0], invs[1]))
        o4 = _dot(jnp.concatenate(e_rows, axis=0), vbds[(c, j)])
        for r in range(2):
            p = 2 * j + r
            o2 = o4[r * WINDOW:(r + 1) * WINDOW] * inv_rows[r]
            g = chunks["sg", j][c * WINDOW:(c + 1) * WINDOW, r * LANES:(r + 1) * LANES]
            os_scr[c * WINDOW:(c + 1) * WINDOW, p * LANES:(p + 1) * LANES] = (o2 * (g * _sigmoid(g))).astype(BF16)
    pk_scr[...] = sk[tm - WINDOW:]
    pv_scr[...] = sv[tm - WINDOW:]

    merged = []
    for cb in range(D_MODEL // 256):
        br_s = _dot(os_scr[...], wbs_ref[:, cb * 256:(cb + 1) * 256])
        merged.append((chunks["mr", cb] * chunks["br", cb] + chunks["ms", cb] * br_s).astype(BF16))
    y_ref[...] = x + _dot(jnp.concatenate(merged, axis=1), wo_ref[...])

    @pl.when(t == pl.num_programs(1) - 1)
    def _():
        st_ref[...] = s_scr[...]
        ko_ref[...] = sk[tm - WINDOW:].T
        vo_ref[...] = sv[tm - WINDOW:].T

    @pl.when(step == last_step)
    def _():
        rg = dproj(C_RG, C_SQ)
        o_r = dor_scr[...]
        parts = []
        for hd in range(RET_HEADS):
            sl = slice(hd * RET_DV, (hd + 1) * RET_DV)
            parts.append(_rms_rows(o_r[:, sl], rng_ref[hd:hd + 1, :]) * (rg[:, sl] * _sigmoid(rg[:, sl])))
        br_r = _dot(jnp.concatenate(parts, axis=1).astype(BF16), wbr_ref[...])
        sg = dproj(C_SG, C_MR)
        br_s = _dot((dos_scr[...] * (sg * _sigmoid(sg))).astype(BF16), wbs_ref[...])
        mrg = _sigmoid(dproj(C_MR, C_MS)) * br_r + _sigmoid(dproj(C_MS, IN_WIDTH)) * br_s
        ys_ref[...] = xs_ref[...] + _dot(mrg.astype(BF16), wo_ref[...])


def _const_spec(shape):
    nd = len(shape)
    return pl.BlockSpec(shape, lambda *_: (0,) * nd, pipeline_mode=pl.Buffered(1))


def _rope_tables(pos, d):
    inv = np.float64(ROPE_THETA) ** (-np.arange(0, d, 2, dtype=np.float64) / d)
    ang = pos.astype(np.float64)[:, None] * inv[None, :]
    c, s = np.cos(ang), np.sin(ang)
    reps = LANES // d
    cos = np.tile(np.concatenate([c, c], axis=1), (1, reps))
    sin = np.tile(np.concatenate([-s, s], axis=1), (1, reps))
    return cos.astype(np.float32), sin.astype(np.float32)


def _log_decay():
    return np.log(1.0 - 2.0 ** (-5.0 - np.arange(RET_HEADS, dtype=np.float64)))


def _group_matrix():
    g = np.arange(256) // SWA_HD
    return jnp.asarray((g[:, None] == g[None, :]).astype(np.float32) / SWA_HD, dtype=BF16)


def _fused_layer(x, xs, state, cache_kt, cache_vt, w, tm):
    B, T, D = x.shape
    nt = T // tm
    nsteps = B * nt
    nb = xs.shape[0]
    assert nb % nsteps == 0 and nb % SUBLANES == 0
    nseq = nb // nsteps
    assert SUBLANES % nseq == 0 and 3 * nseq <= SUBLANES

    lg = _log_decay()
    n = np.arange(tm, dtype=np.float64)
    diff = n[:, None] - n[None, :]
    dmask = np.where(diff[None] >= 0, np.exp(np.maximum(diff, 0.0)[None] * lg[:, None, None]), 0.0).astype(np.float32)
    cross = np.broadcast_to(np.exp((n[None, :] + 1.0) * lg[:, None])[:, :, None], (RET_HEADS, tm, RET_DK))
    kdec = np.broadcast_to(np.exp((tm - 1.0 - n)[None, :] * lg[:, None])[:, :, None], (RET_HEADS, tm, RET_DK))
    decay_chunk = tuple(float(v) for v in np.exp(tm * lg))
    decay_step = tuple(float(v) for v in np.exp(lg))
    pos = np.arange(T)
    rcos, rsin = _rope_tables(pos, RET_DK)
    scos, ssin = _rope_tables(pos, SWA_HD)
    dpos = np.full((1,), PAST_LEN)
    drcos, drsin = _rope_tables(dpos, RET_DK)
    dscos, dssin = _rope_tables(dpos, SWA_HD)

    tok = lambda b, t: (b, t, 0)
    tab = lambda b, t: (t, 0)
    seq4 = lambda b, t: (b * nt + t, 0, 0, 0)
    seq3 = lambda b, t: (b * nt + t, 0, 0)
    in_specs = [
        pl.BlockSpec((None, tm, D), tok),
        _const_spec((1, D)),
        pl.BlockSpec(memory_space=pltpu.HBM),
        _const_spec((RET_HEADS, RET_DV)),
        _const_spec((1, 256)),
        _const_spec((1, 256)),
        pl.BlockSpec(memory_space=pltpu.SMEM),
        pl.BlockSpec(memory_space=pltpu.HBM),
        pl.BlockSpec(memory_space=pltpu.HBM),
        pl.BlockSpec(memory_space=pltpu.HBM),
        pl.BlockSpec((tm, LANES), tab),
        pl.BlockSpec((tm, LANES), tab),
        pl.BlockSpec((tm, LANES), tab),
        pl.BlockSpec((tm, LANES), tab),
        _const_spec((RET_HEADS, tm, tm)),
        _const_spec((RET_HEADS, tm, RET_DK)),
        _const_spec((RET_HEADS, tm, RET_DK)),
        _const_spec((256, 256)),
        _const_spec((nb, D)),
        _const_spec((1, LANES)),
        _const_spec((1, LANES)),
        _const_spec((1, LANES)),
        _const_spec((1, LANES)),
        pl.BlockSpec((nseq, RET_HEADS, RET_DK, RET_DV), seq4),
        pl.BlockSpec((nseq, SWA_KV, WINDOW), seq3),
        pl.BlockSpec((nseq, SWA_KV, WINDOW), seq3),
    ]
    out_shape = (
        jax.ShapeDtypeStruct((B, T, D), F32),
        jax.ShapeDtypeStruct((B, RET_HEADS, RET_DK, RET_DV), F32),
        jax.ShapeDtypeStruct((B, SWA_KV, WINDOW), F32),
        jax.ShapeDtypeStruct((B, SWA_KV, WINDOW), F32),
        jax.ShapeDtypeStruct((nb, D), F32),
        jax.ShapeDtypeStruct(state.shape, F32),
        jax.ShapeDtypeStruct(cache_kt.shape, F32),
        jax.ShapeDtypeStruct(cache_vt.shape, F32),
    )
    out_specs = (
        pl.BlockSpec((None, tm, D), tok),
        pl.BlockSpec((None, RET_HEADS, RET_DK, RET_DV), lambda b, t: (b, 0, 0, 0)),
        pl.BlockSpec((None, SWA_KV, WINDOW), lambda b, t: (b, 0, 0)),
        pl.BlockSpec((None, SWA_KV, WINDOW), lambda b, t: (b, 0, 0)),
        pl.BlockSpec((nb, D), lambda b, t: (0, 0)),
        pl.BlockSpec((nseq, RET_HEADS, RET_DK, RET_DV), seq4),
        pl.BlockSpec((nseq, SWA_KV, WINDOW), seq3),
        pl.BlockSpec((nseq, SWA_KV, WINDOW), seq3),
    )
    scratch = [
        pltpu.VMEM((RET_HEADS, RET_DK, RET_DV), F32),
        pltpu.VMEM((WINDOW, SWA_KV), F32),
        pltpu.VMEM((WINDOW, SWA_KV), F32),
        pltpu.VMEM((tm, RET_V), BF16),
        pltpu.VMEM((tm, SWA_Q), BF16),
        pltpu.VMEM((nb, RET_QK), F32),
        pltpu.VMEM((nb, RET_QK), F32),
        pltpu.VMEM((nb, RET_V), F32),
        pltpu.VMEM((nb, SWA_Q), F32),
        pltpu.VMEM((nb, SWA_KV), F32),
        pltpu.VMEM((nb, SWA_KV), F32),
        pltpu.VMEM((nb, RET_V), F32),
        pltpu.VMEM((nb, SWA_Q), F32),
        pltpu.VMEM((D, IN_WIDTH), BF16),
        pltpu.VMEM((RET_V, D), BF16),
        pltpu.VMEM((SWA_Q, D), BF16),
        pltpu.VMEM((D, D), BF16),
        pltpu.VMEM((2, D, WCHUNK), F32),
        pltpu.SemaphoreType.DMA((2,)),
    ]
    return pl.pallas_call(
        functools.partial(_layer_kernel, decay_chunk=decay_chunk, decay_step=decay_step),
        grid=(B, nt),
        in_specs=in_specs,
        out_specs=out_specs,
        out_shape=out_shape,
        scratch_shapes=scratch,
        compiler_params=pltpu.CompilerParams(
            dimension_semantics=("arbitrary", "arbitrary"), vmem_limit_bytes=VMEM_LIMIT),
        name="hybrid_layer",
    )(x, w["ng"], w["win"], w["rng"], w["qg"], w["kg"], w["sinks"], w["wbr"], w["wbs"], w["wo"],
      rcos, rsin, scos, ssin, dmask, cross.astype(np.float32), kdec.astype(np.float32), _group_matrix(),
      xs, drcos, drsin, dscos, dssin, state, cache_kt, cache_vt)


def kernel(x_prompt, x_sample, state_ret, cache_swa_k, cache_swa_v, norm_g, w_in, ret_norm_g, swa_q_g, swa_k_g,
           swa_sinks, w_br_ret, w_br_swa, w_out):
    depth = norm_g.shape[0]
    assert depth == 1 and x_sample.shape[1] == 1
    B, T, D = x_prompt.shape
    nb = x_sample.shape[0]
    wc = cache_swa_k.shape[2]
    assert wc == WINDOW and nb == LANES and T % PROMPT_TM == 0
    l = 0
    w = {
        "ng": norm_g[l][None, :],
        "win": w_in[l],
        "rng": ret_norm_g[l],
        "qg": jnp.tile(swa_q_g[l], 256 // SWA_HD)[None, :],
        "kg": jnp.tile(swa_k_g[l], 256 // SWA_HD)[None, :],
        "sinks": swa_sinks[l],
        "wbr": w_br_ret[l],
        "wbs": w_br_swa[l],
        "wo": w_out[l],
    }
    to_t = lambda c: jnp.transpose(c.reshape(nb, wc, SWA_KV), (0, 2, 1))
    from_t = lambda c: jnp.transpose(c, (0, 2, 1)).reshape(1, c.shape[0], wc, SWA_KV_HEADS, SWA_HD)
    yp, rp, kp, vp, ys, rs, ks, vs = _fused_layer(
        x_prompt, x_sample[:, 0, :], state_ret[l], to_t(cache_swa_k[l]), to_t(cache_swa_v[l]), w, PROMPT_TM)
    return (yp, ys[:, None, :], rp[None], rs[None],
            from_t(kp), from_t(vp), from_t(ks), from_t(vs))
```

```python
import functools

import numpy as np
import jax
import jax.numpy as jnp
from jax import lax
from jax.experimental import pallas as pl
from jax.experimental.pallas import tpu as pltpu

F32 = jnp.float32
BF16 = jnp.bfloat16

D_MODEL = 1024
RET_HEADS = 4
RET_DK = 128
RET_DV = 256
RET_QK = RET_HEADS * RET_DK
RET_V = RET_HEADS * RET_DV
SWA_HEADS = 16
SWA_KV_HEADS = 4
SWA_HD = 64
SWA_Q = SWA_HEADS * SWA_HD
SWA_KV = SWA_KV_HEADS * SWA_HD
WINDOW = 128
ROPE_THETA = 10000.0
EPS = 1e-6
PAST_LEN = 8192

C_RQ = 0
C_RK = C_RQ + RET_QK
C_RV = C_RK + RET_QK
C_RG = C_RV + RET_V
C_SQ = C_RG + RET_V
C_SK = C_SQ + SWA_Q
C_SV = C_SK + SWA_KV
C_SG = C_SV + SWA_KV
C_MR = C_SG + SWA_Q
C_MS = C_MR + D_MODEL
IN_WIDTH = C_MS + D_MODEL

LANES = 128
SUBLANES = 8
MXU_N = 256
PROMPT_TM = 256
WCHUNK = 512
VMEM_LIMIT = 60 * 1024 * 1024


def _dot(a, b):
    return jnp.dot(a, b, preferred_element_type=F32)


def _dot_nt(a, b):
    return lax.dot_general(a, b, (((1,), (1,)), ((), ())), preferred_element_type=F32)


def _dot_tn(a, b):
    return lax.dot_general(a, b, (((0,), (0,)), ((), ())), preferred_element_type=F32)


def _sigmoid(x):
    return 1.0 / (1.0 + jnp.exp(-x))


def _rms_rows(x, g):
    return x * lax.rsqrt(jnp.mean(x * x, axis=-1, keepdims=True) + EPS) * g


def _group_mean_sq(x, gmat):
    x2 = x * x
    hi = x2.astype(BF16)
    lo = (x2 - hi.astype(F32)).astype(BF16)
    return _dot(hi, gmat) + _dot(lo, gmat)


def _rope128(x, cos, sin_signed):
    return x * cos + pltpu.roll(x, 64, 1) * sin_signed


def _rope64(x, cos, sin_signed, first_half):
    rot = jnp.where(first_half, pltpu.roll(x, 96, 1), pltpu.roll(x, 32, 1))
    return x * cos + rot * sin_signed


def _swa_norm_rope(x, gmat, gain, cos, sin_signed, first_half):
    xn = x * lax.rsqrt(_group_mean_sq(x, gmat) + EPS) * gain
    cols = [_rope64(xn[:, c * LANES:(c + 1) * LANES], cos, sin_signed, first_half) for c in range(2)]
    return jnp.concatenate(cols, axis=1)


def _heads_to_rows(qrow):
    row = lax.broadcasted_iota(jnp.int32, (SUBLANES, LANES), 0)
    half = lax.broadcasted_iota(jnp.int32, (SUBLANES, LANES), 1) // SWA_HD
    zeros = jnp.zeros((SUBLANES, LANES), F32)
    tiles = []
    for jt in range(2):
        acc = zeros
        for r in range(8):
            sc = 4 * jt + r // 2
            piece = jnp.broadcast_to(qrow[:, sc * LANES:(sc + 1) * LANES], (SUBLANES, LANES))
            if r % 2 != r // 4:
                piece = pltpu.roll(piece, SWA_HD, 1)
            acc = jnp.where((row == r) & (half == r // 4), piece, acc)
        tiles.append(acc)
    return jnp.concatenate([jnp.concatenate([tiles[0], zeros], axis=1),
                            jnp.concatenate([zeros, tiles[1]], axis=1)], axis=0)


def _rows_to_heads(o):
    low = lax.broadcasted_iota(jnp.int32, (1, LANES), 1) < SWA_HD
    cols = []
    for tc in range(SWA_HEADS // 2):
        pieces = []
        for e in range(2):
            h = 2 * tc + e
            j = h // 4
            piece = o[h:h + 1, (j // 2) * LANES:(j // 2 + 1) * LANES]
            if j % 2 != e:
                piece = pltpu.roll(piece, SWA_HD, 1)
            pieces.append(piece)
        cols.append(jnp.where(low, pieces[0], pieces[1]))
    return jnp.concatenate(cols, axis=1)


def _load_weights_bf16(srcs, dsts, stage, sem):
    jobs = [(src, dst, c0) for src, dst in zip(srcs, dsts) for c0 in range(0, src.shape[1], WCHUNK)]

    def copy(i):
        src, _, c0 = jobs[i]
        return pltpu.make_async_copy(src.at[:, pl.ds(c0, WCHUNK)], stage.at[i % 2], sem.at[i % 2])

    copy(0).start()
    for i, (_, dst, c0) in enumerate(jobs):
        if i + 1 < len(jobs):
            copy(i + 1).start()
        copy(i).wait()
        dst[:, c0:c0 + WCHUNK] = stage[i % 2].astype(BF16)


def _layer_kernel(x_ref, ng_ref, win_hbm, rng_ref, qg_ref, kg_ref, sink_ref, wbr_hbm, wbs_hbm, wo_hbm,
                  rope_ref, dmask_ref, cross_ref, kdec_ref, gmat_ref,
                  xs_ref, drcos_ref, drsin_ref, dscos_ref, dssin_ref, dst_ref, dkt_ref, dvt_ref,
                  y_ref, st_ref, ko_ref, vo_ref, ys_ref, dsto_ref, dkto_ref, dvto_ref,
                  s_scr, pk_scr, pv_scr, or_scr, os_scr,
                  drq_scr, drk_scr, drv_scr, dsq_scr, dsk_scr, dsv_scr, dor_scr, dos_scr,
                  win_ref, wbr_ref, wbs_ref, wo_ref, wstage, wsem,
                  *, decay_chunk, decay_step):
    t = pl.program_id(1)
    step = pl.program_id(0) * pl.num_programs(1) + t
    last_step = pl.num_programs(0) * pl.num_programs(1) - 1
    tm = x_ref.shape[0]
    nblk = tm // WINDOW
    nseq = dst_ref.shape[0]

    gmat = gmat_ref[...]
    lane = lax.broadcasted_iota(jnp.int32, (1, LANES), 1)
    first_half = (lane % SWA_HD) < (SWA_HD // 2)
    low = lane < SWA_HD

    @pl.when(step == 0)
    def _():
        _load_weights_bf16((win_hbm, wbr_hbm, wbs_hbm, wo_hbm), (win_ref, wbr_ref, wbs_ref, wo_ref), wstage, wsem)

    def dproj(lo, hi):
        hs = _rms_rows(xs_ref[...], ng_ref[...]).astype(BF16)
        return _dot(hs, win_ref[:, lo:hi])

    @pl.when(step == 0)
    def _():
        drcos = drcos_ref[...]
        drsin = drsin_ref[...]
        rq = dproj(C_RQ, C_RK)
        rk = dproj(C_RK, C_RV)
        for hd in range(RET_HEADS):
            sl = slice(hd * RET_DK, (hd + 1) * RET_DK)
            drq_scr[:, sl] = _rope128(rq[:, sl], drcos, drsin)
            drk_scr[:, sl] = _rope128(rk[:, sl], drcos, drsin) * (RET_DK ** -0.5)
        drv_scr[...] = dproj(C_RV, C_RG)
        dscos = dscos_ref[...]
        dssin = dssin_ref[...]
        for cb in range(SWA_Q // MXU_N):
            dsq_scr[:, cb * MXU_N:(cb + 1) * MXU_N] = _swa_norm_rope(
                dproj(C_SQ + cb * MXU_N, C_SQ + (cb + 1) * MXU_N), gmat, qg_ref[...], dscos, dssin,
                first_half) * (SWA_HD ** -0.5)
        dsk_scr[...] = _swa_norm_rope(dproj(C_SK, C_SV), gmat, kg_ref[...], dscos, dssin, first_half)
        dsv_scr[...] = dproj(C_SV, C_SG)

    @pl.when(t == 0)
    def _():
        s_scr[...] = jnp.zeros_like(s_scr)
        pk_scr[...] = jnp.zeros_like(pk_scr)
        pv_scr[...] = jnp.zeros_like(pv_scr)

    seq0 = step * nseq
    grp = pl.multiple_of((seq0 // SUBLANES) * SUBLANES, SUBLANES)
    rows8 = pl.ds(grp, SUBLANES)
    row8 = lax.broadcasted_iota(jnp.int32, (SUBLANES, 1), 0)
    newest = lane == WINDOW - 1
    dstate = {}

    def column_broadcast(rows_scr):
        rowi = lax.broadcasted_iota(jnp.int32, (SUBLANES, SWA_KV), 0)
        ones_row = lax.broadcasted_iota(jnp.int32, (SUBLANES, nseq * LANES), 0)
        ones_lane = lax.broadcasted_iota(jnp.int32, (SUBLANES, nseq * LANES), 1)
        tile = jnp.zeros((SUBLANES, SWA_KV), F32)
        for s in range(nseq):
            r = rows_scr[pl.ds(seq0 + s, 1), :]
            hi = r.astype(BF16).astype(F32)
            mid = (r - hi).astype(BF16).astype(F32)
            lo = (r - hi) - mid
            for i, term in enumerate((hi, mid, lo)):
                tile = jnp.where(rowi == 3 * s + i, jnp.broadcast_to(term, (SUBLANES, SWA_KV)), tile)
        pick = (ones_row // 3 == ones_lane // LANES) & (ones_row < 3 * nseq)
        return _dot_tn(tile.astype(BF16), jnp.where(pick, 1.0, 0.0).astype(BF16))

    def decode_scores():
        q8 = drq_scr[rows8, :]
        k8 = drk_scr[rows8, :]
        v8 = drv_scr[rows8, :]
        mine = (row8 >= seq0 - grp) & (row8 < seq0 - grp + nseq)
        o_acc = [jnp.zeros((SUBLANES, RET_DV), F32)] * RET_HEADS
        knew = column_broadcast(dsk_scr)
        vnew = column_broadcast(dsv_scr)
        scs, vts = [], []
        for s in range(nseq):
            b = seq0 + s
            sel = row8 == (b - grp)
            for hd in range(RET_HEADS):
                qh = q8[:, hd * RET_DK:(hd + 1) * RET_DK]
                kh = k8[:, hd * RET_DK:(hd + 1) * RET_DK]
                vh = v8[:, hd * RET_DV:(hd + 1) * RET_DV]
                state = dst_ref[s, hd]
                inter = _dot(qh.astype(BF16), state.astype(BF16))
                outer = _dot_tn(jnp.where(sel, kh, 0.0).astype(BF16), vh.astype(BF16))
                dsto_ref[s, hd] = decay_step[hd] * state + outer
                o_acc[hd] = jnp.where(sel, inter, o_acc[hd])
            kt_new = jnp.where(newest, knew[:, s * LANES:(s + 1) * LANES], pltpu.roll(dkt_ref[s], WINDOW - 1, 1))
            vt_new = jnp.where(newest, vnew[:, s * LANES:(s + 1) * LANES], pltpu.roll(dvt_ref[s], WINDOW - 1, 1))
            dkto_ref[s] = kt_new
            dvto_ref[s] = vt_new
            qexp = _heads_to_rows(dsq_scr[pl.ds(b, 1), :]).astype(BF16)
            scs.append(_dot(qexp, kt_new.astype(BF16)))
            vts.append(vt_new.astype(BF16))
        for hd in range(RET_HEADS):
            qh = q8[:, hd * RET_DK:(hd + 1) * RET_DK]
            kh = k8[:, hd * RET_DK:(hd + 1) * RET_DK]
            vh = v8[:, hd * RET_DV:(hd + 1) * RET_DV]
            sl = slice(hd * RET_DV, (hd + 1) * RET_DV)
            o = jnp.sum(qh * kh, axis=-1, keepdims=True) * vh + decay_step[hd] * o_acc[hd]
            dor_scr[rows8, sl] = jnp.where(mine, o, dor_scr[rows8, sl])
        dstate["scs"], dstate["vts"] = scs, vts

    def decode_attend():
        head = lax.broadcasted_iota(jnp.int32, (SWA_HEADS, 1), 0)
        sink_col = jnp.zeros((SWA_HEADS, 1), F32)
        for hh in range(SWA_HEADS):
            sink_col = jnp.where(head == hh, sink_ref[hh], sink_col)
        outs = []
        for s in range(nseq):
            sc = dstate["scs"][s]
            m = jnp.maximum(jnp.max(sc, axis=-1, keepdims=True), sink_col)
            e = jnp.exp(sc - m)
            p = e / (jnp.sum(e, axis=-1, keepdims=True) + jnp.exp(sink_col - m))
            pt = jnp.concatenate([p.astype(BF16), jnp.zeros((LANES - SWA_HEADS, WINDOW), BF16)], axis=0)
            out_t = _dot_nt(dstate["vts"][s], pt)
            outs.append(out_t.T[:SWA_HEADS])
        dstate["outs"] = outs

    def decode_store():
        for s in range(nseq):
            dos_scr[pl.ds(seq0 + s, 1), :] = _rows_to_heads(dstate["outs"][s])

    x = x_ref[...]
    h = _rms_rows(x, ng_ref[...]).astype(BF16)

    def proj(lo, hi):
        return _dot(h, win_ref[:, lo:hi])

    rcos, rsin, scos, ssin = (rope_ref[:, i * LANES:(i + 1) * LANES] for i in range(4))

    rq = proj(C_RQ, C_RK)
    decode_scores()
    rk = proj(C_RK, C_RV)
    rv = proj(C_RV, C_RG).astype(BF16)
    decode_attend()
    k_scale = RET_DK ** -0.5
    scores, inter, vs = [], [], []
    for hd in range(RET_HEADS):
        q = _rope128(rq[:, hd * RET_DK:(hd + 1) * RET_DK], rcos, rsin)
        k = _rope128(rk[:, hd * RET_DK:(hd + 1) * RET_DK], rcos, rsin) * k_scale
        v = rv[:, hd * RET_DV:(hd + 1) * RET_DV]
        state = s_scr[hd]
        scores.append(_dot_nt(q.astype(BF16), k.astype(BF16)))
        inter.append(_dot((q * cross_ref[hd]).astype(BF16), state.astype(BF16)))
        s_scr[hd] = decay_chunk[hd] * state + _dot_tn((k * kdec_ref[hd]).astype(BF16), v)
        vs.append(v)
    decode_store()
    sq_cols = []
    sk = sv = None
    for hd in range(RET_HEADS):
        g = proj(C_RG + hd * RET_DV, C_RG + (hd + 1) * RET_DV)
        sq_cols.append(_swa_norm_rope(proj(C_SQ + hd * MXU_N, C_SQ + (hd + 1) * MXU_N), gmat, qg_ref[...], scos, ssin,
                                      first_half) * (SWA_HD ** -0.5))
        if hd == 0:
            sk = _swa_norm_rope(proj(C_SK, C_SV), gmat, kg_ref[...], scos, ssin, first_half)
        if hd == 1:
            sv = proj(C_SV, C_SG)
        o = _dot((scores[hd] * dmask_ref[hd]).astype(BF16), vs[hd]) + inter[hd]
        on = _rms_rows(o, rng_ref[hd:hd + 1, :])
        or_scr[:, hd * RET_DV:(hd + 1) * RET_DV] = (on * (g * _sigmoid(g))).astype(BF16)

    row_i = lax.broadcasted_iota(jnp.int32, (WINDOW, 2 * WINDOW), 0)
    col_j = lax.broadcasted_iota(jnp.int32, (WINDOW, 2 * WINDOW), 1)
    allowed, kbds, vbds = [], {}, {}
    for c in range(nblk):
        lower = row_i + 1
        if c == 0:
            lower = jnp.maximum(lower, jnp.where(t == 0, WINDOW, 0))
        allowed.append((col_j >= lower) & (col_j <= row_i + WINDOW))

    def block_diag(c, j, src, prev_scr):
        r0 = c * WINDOW
        prev = prev_scr[...] if c == 0 else src[r0 - WINDOW:r0]
        cat = jnp.concatenate([prev, src[r0:r0 + WINDOW]], axis=0)
        col, half = divmod(j, 2)
        a = cat[:, col * LANES:(col + 1) * LANES]
        ar = pltpu.roll(a, SWA_HD, 1)
        if half == 0:
            bd = jnp.concatenate([jnp.where(low, a, 0.0), jnp.where(low, 0.0, ar)], axis=0)
        else:
            bd = jnp.concatenate([jnp.where(low, ar, 0.0), jnp.where(low, 0.0, a)], axis=0)
        return bd.astype(BF16)

    quads = [(c, j) for j in range(SWA_KV_HEADS) for c in range(nblk)]

    def qk(i):
        c, j = quads[i]
        kbds[(c, j)] = block_diag(c, j, sk, pk_scr)
        vbds[(c, j)] = block_diag(c, j, sv, pv_scr)
        qp = jnp.concatenate([sq_cols[j][c * WINDOW:(c + 1) * WINDOW, r * LANES:(r + 1) * LANES] for r in range(2)],
                             axis=0).astype(BF16)
        return _dot_nt(qp, kbds[(c, j)])

    chunks = {}

    def filler(i):
        kind, cb = divmod(i, 4)
        if kind == 0:
            chunks["sg", cb] = proj(C_SG + cb * MXU_N, C_SG + (cb + 1) * MXU_N)
        elif kind == 1:
            chunks["mr", cb] = _sigmoid(proj(C_MR + cb * MXU_N, C_MR + (cb + 1) * MXU_N))
        elif kind == 2:
            chunks["ms", cb] = _sigmoid(proj(C_MS + cb * MXU_N, C_MS + (cb + 1) * MXU_N))
        else:
            chunks["br", cb] = _dot(or_scr[...], wbr_ref[:, cb * MXU_N:(cb + 1) * MXU_N])

    order = [0, 4, 8, 12, 1, 5, 9, 13, 2, 6, 10, 14, 3, 7, 11, 15]
    per_unit = len(order) // len(quads)
    s4_next = qk(0)
    for i, (c, j) in enumerate(quads):
        s4 = s4_next
        if i + 1 < len(quads):
            s4_next = qk(i + 1)
        for f in order[i * per_unit:(i + 1) * per_unit]:
            filler(f)
        e_rows, inv_rows = [], []
        for r in range(2):
            es, invs = [], []
            for a in range(2):
                sink = sink_ref[4 * j + 2 * r + a]
                s = jnp.where(allowed[c], s4[r * WINDOW:(r + 1) * WINDOW, a * 2 * WINDOW:(a + 1) * 2 * WINDOW], -jnp.inf)
                m = jnp.maximum(jnp.max(s, axis=-1, keepdims=True), sink)
                e = jnp.exp(s - m)
                invs.append(1.0 / (jnp.sum(e, axis=-1, keepdims=True) + jnp.exp(sink - m)))
                es.append(e.astype(BF16))
            e_rows.append(jnp.concatenate(es, axis=1))
            inv_rows.append(jnp.where(low, invs[0], invs[1]))
        o4 = _dot(jnp.concatenate(e_rows, axis=0), vbds[(c, j)])
        for r in range(2):
            p = 2 * j + r
            o2 = o4[r * WINDOW:(r + 1) * WINDOW] * inv_rows[r]
            g = chunks["sg", j][c * WINDOW:(c + 1) * WINDOW, r * LANES:(r + 1) * LANES]
            os_scr[c * WINDOW:(c + 1) * WINDOW, p * LANES:(p + 1) * LANES] = (o2 * (g * _sigmoid(g))).astype(BF16)
    pk_scr[...] = sk[tm - WINDOW:]
    pv_scr[...] = sv[tm - WINDOW:]

    merged = []
    for cb in range(D_MODEL // MXU_N):
        br_s = _dot(os_scr[...], wbs_ref[:, cb * MXU_N:(cb + 1) * MXU_N])
        merged.append((chunks["mr", cb] * chunks["br", cb] + chunks["ms", cb] * br_s).astype(BF16))
    y_ref[...] = x + _dot(jnp.concatenate(merged, axis=1), wo_ref[...])

    @pl.when(t == pl.num_programs(1) - 1)
    def _():
        st_ref[...] = s_scr[...]
        ko_ref[...] = sk[tm - WINDOW:].T
        vo_ref[...] = sv[tm - WINDOW:].T

    @pl.when(step == last_step)
    def _():
        rg = dproj(C_RG, C_SQ)
        o_r = dor_scr[...]
        parts = []
        for hd in range(RET_HEADS):
            sl = slice(hd * RET_DV, (hd + 1) * RET_DV)
            parts.append(_rms_rows(o_r[:, sl], rng_ref[hd:hd + 1, :]) * (rg[:, sl] * _sigmoid(rg[:, sl])))
        br_r = _dot(jnp.concatenate(parts, axis=1).astype(BF16), wbr_ref[...])
        sg = dproj(C_SG, C_MR)
        br_s = _dot((dos_scr[...] * (sg * _sigmoid(sg))).astype(BF16), wbs_ref[...])
        mrg = _sigmoid(dproj(C_MR, C_MS)) * br_r + _sigmoid(dproj(C_MS, IN_WIDTH)) * br_s
        ys_ref[...] = xs_ref[...] + _dot(mrg.astype(BF16), wo_ref[...])


def _const_spec(shape):
    nd = len(shape)
    return pl.BlockSpec(shape, lambda *_: (0,) * nd, pipeline_mode=pl.Buffered(1))


def _rope_tables(pos, d):
    inv = np.float64(ROPE_THETA) ** (-np.arange(0, d, 2, dtype=np.float64) / d)
    ang = pos.astype(np.float64)[:, None] * inv[None, :]
    c, s = np.cos(ang), np.sin(ang)
    reps = LANES // d
    cos = np.tile(np.concatenate([c, c], axis=1), (1, reps))
    sin = np.tile(np.concatenate([-s, s], axis=1), (1, reps))
    return cos.astype(np.float32), sin.astype(np.float32)


def _log_decay():
    return np.log(1.0 - 2.0 ** (-5.0 - np.arange(RET_HEADS, dtype=np.float64)))


def _group_matrix():
    g = np.arange(MXU_N) // SWA_HD
    return jnp.asarray((g[:, None] == g[None, :]).astype(np.float32) / SWA_HD, dtype=BF16)


def _fused_layer(x, xs, state, cache_kt, cache_vt, w, tm):
    B, T, D = x.shape
    nt = T // tm
    nsteps = B * nt
    nb = xs.shape[0]
    assert nb % nsteps == 0 and nb % SUBLANES == 0
    nseq = nb // nsteps
    assert SUBLANES % nseq == 0 and 3 * nseq <= SUBLANES

    lg = _log_decay()
    n = np.arange(tm, dtype=np.float64)
    diff = n[:, None] - n[None, :]
    dmask = np.where(diff[None] >= 0, np.exp(np.maximum(diff, 0.0)[None] * lg[:, None, None]), 0.0).astype(np.float32)
    cross = np.broadcast_to(np.exp((n[None, :] + 1.0) * lg[:, None])[:, :, None], (RET_HEADS, tm, RET_DK))
    kdec = np.broadcast_to(np.exp((tm - 1.0 - n)[None, :] * lg[:, None])[:, :, None], (RET_HEADS, tm, RET_DK))
    decay_chunk = tuple(float(v) for v in np.exp(tm * lg))
    decay_step = tuple(float(v) for v in np.exp(lg))
    pos = np.arange(T)
    rcos, rsin = _rope_tables(pos, RET_DK)
    scos, ssin = _rope_tables(pos, SWA_HD)
    dpos = np.full((1,), PAST_LEN)
    drcos, drsin = _rope_tables(dpos, RET_DK)
    dscos, dssin = _rope_tables(dpos, SWA_HD)

    tok = lambda b, t: (b, t, 0)
    tab = lambda b, t: (t, 0)
    seq4 = lambda b, t: (b * nt + t, 0, 0, 0)
    seq3 = lambda b, t: (b * nt + t, 0, 0)
    in_specs = [
        pl.BlockSpec((None, tm, D), tok),
        _const_spec((1, D)),
        pl.BlockSpec(memory_space=pltpu.HBM),
        _const_spec((RET_HEADS, RET_DV)),
        _const_spec((1, MXU_N)),
        _const_spec((1, MXU_N)),
        pl.BlockSpec(memory_space=pltpu.SMEM),
        pl.BlockSpec(memory_space=pltpu.HBM),
        pl.BlockSpec(memory_space=pltpu.HBM),
        pl.BlockSpec(memory_space=pltpu.HBM),
        pl.BlockSpec((tm, 4 * LANES), tab),
        _const_spec((RET_HEADS, tm, tm)),
        _const_spec((RET_HEADS, tm, RET_DK)),
        _const_spec((RET_HEADS, tm, RET_DK)),
        _const_spec((MXU_N, MXU_N)),
        _const_spec((nb, D)),
        _const_spec((1, LANES)),
        _const_spec((1, LANES)),
        _const_spec((1, LANES)),
        _const_spec((1, LANES)),
        pl.BlockSpec((nseq, RET_HEADS, RET_DK, RET_DV), seq4),
        pl.BlockSpec((nseq, SWA_KV, WINDOW), seq3),
        pl.BlockSpec((nseq, SWA_KV, WINDOW), seq3),
    ]
    out_shape = (
        jax.ShapeDtypeStruct((B, T, D), F32),
        jax.ShapeDtypeStruct((B, RET_HEADS, RET_DK, RET_DV), F32),
        jax.ShapeDtypeStruct((B, SWA_KV, WINDOW), F32),
        jax.ShapeDtypeStruct((B, SWA_KV, WINDOW), F32),
        jax.ShapeDtypeStruct((nb, D), F32),
        jax.ShapeDtypeStruct(state.shape, F32),
        jax.ShapeDtypeStruct(cache_kt.shape, F32),
        jax.ShapeDtypeStruct(cache_vt.shape, F32),
    )
    out_specs = (
        pl.BlockSpec((None, tm, D), tok),
        pl.BlockSpec((None, RET_HEADS, RET_DK, RET_DV), lambda b, t: (b, 0, 0, 0)),
        pl.BlockSpec((None, SWA_KV, WINDOW), lambda b, t: (b, 0, 0)),
        pl.BlockSpec((None, SWA_KV, WINDOW), lambda b, t: (b, 0, 0)),
        pl.BlockSpec((nb, D), lambda b, t: (0, 0)),
        pl.BlockSpec((nseq, RET_HEADS, RET_DK, RET_DV), seq4),
        pl.BlockSpec((nseq, SWA_KV, WINDOW), seq3),
        pl.BlockSpec((nseq, SWA_KV, WINDOW), seq3),
    )
    scratch = [
        pltpu.VMEM((RET_HEADS, RET_DK, RET_DV), F32),
        pltpu.VMEM((WINDOW, SWA_KV), F32),
        pltpu.VMEM((WINDOW, SWA_KV), F32),
        pltpu.VMEM((tm, RET_V), BF16),
        pltpu.VMEM((tm, SWA_Q), BF16),
        pltpu.VMEM((nb, RET_QK), F32),
        pltpu.VMEM((nb, RET_QK), F32),
        pltpu.VMEM((nb, RET_V), F32),
        pltpu.VMEM((nb, SWA_Q), F32),
        pltpu.VMEM((nb, SWA_KV), F32),
        pltpu.VMEM((nb, SWA_KV), F32),
        pltpu.VMEM((nb, RET_V), F32),
        pltpu.VMEM((nb, SWA_Q), F32),
        pltpu.VMEM((D, IN_WIDTH), BF16),
        pltpu.VMEM((RET_V, D), BF16),
        pltpu.VMEM((SWA_Q, D), BF16),
        pltpu.VMEM((D, D), BF16),
        pltpu.VMEM((2, D, WCHUNK), F32),
        pltpu.SemaphoreType.DMA((2,)),
    ]
    return pl.pallas_call(
        functools.partial(_layer_kernel, decay_chunk=decay_chunk, decay_step=decay_step),
        grid=(B, nt),
        in_specs=in_specs,
        out_specs=out_specs,
        out_shape=out_shape,
        scratch_shapes=scratch,
        compiler_params=pltpu.CompilerParams(
            dimension_semantics=("arbitrary", "arbitrary"), vmem_limit_bytes=VMEM_LIMIT),
        name="hybrid_layer",
    )(x, w["ng"], w["win"], w["rng"], w["qg"], w["kg"], w["sinks"], w["wbr"], w["wbs"], w["wo"],
      np.concatenate([rcos, rsin, scos, ssin], axis=1), dmask, cross.astype(np.float32), kdec.astype(np.float32),
      _group_matrix(),
      xs, drcos, drsin, dscos, dssin, state, cache_kt, cache_vt)


def kernel(x_prompt, x_sample, state_ret, cache_swa_k, cache_swa_v, norm_g, w_in, ret_norm_g, swa_q_g, swa_k_g,
           swa_sinks, w_br_ret, w_br_swa, w_out):
    depth = norm_g.shape[0]
    assert depth == 1 and x_sample.shape[1] == 1
    B, T, D = x_prompt.shape
    nb = x_sample.shape[0]
    wc = cache_swa_k.shape[2]
    assert wc == WINDOW and T % PROMPT_TM == 0
    l = 0
    w = {
        "ng": norm_g[l][None, :],
        "win": w_in[l],
        "rng": ret_norm_g[l],
        "qg": jnp.tile(swa_q_g[l], MXU_N // SWA_HD)[None, :],
        "kg": jnp.tile(swa_k_g[l], MXU_N // SWA_HD)[None, :],
        "sinks": swa_sinks[l],
        "wbr": w_br_ret[l],
        "wbs": w_br_swa[l],
        "wo": w_out[l],
    }
    to_t = lambda c: jnp.transpose(c.reshape(nb, wc, SWA_KV), (0, 2, 1))
    from_t = lambda c: jnp.transpose(c, (0, 2, 1)).reshape(1, c.shape[0], wc, SWA_KV_HEADS, SWA_HD)
    yp, rp, kp, vp, ys, rs, ks, vs = _fused_layer(
        x_prompt, x_sample[:, 0, :], state_ret[l], to_t(cache_swa_k[l]), to_t(cache_swa_v[l]), w, PROMPT_TM)
    return (yp, ys[:, None, :], rp[None], rs[None],
            from_t(kp), from_t(vp), from_t(ks), from_t(vs))
```

```python
import functools

import numpy as np
import jax
import jax.numpy as jnp
from jax import lax
from jax.experimental import pallas as pl
from jax.experimental.pallas import tpu as pltpu

F32 = jnp.float32
BF16 = jnp.bfloat16

D_MODEL = 1024
RET_HEADS = 4
RET_DK = 128
RET_DV = 256
RET_QK = RET_HEADS * RET_DK
RET_V = RET_HEADS * RET_DV
SWA_HEADS = 16
SWA_KV_HEADS = 4
SWA_HD = 64
SWA_Q = SWA_HEADS * SWA_HD
SWA_KV = SWA_KV_HEADS * SWA_HD
WINDOW = 128
ROPE_THETA = 10000.0
EPS = 1e-6
PAST_LEN = 8192

C_RQ = 0
C_RK = C_RQ + RET_QK
C_RV = C_RK + RET_QK
C_RG = C_RV + RET_V
C_SQ = C_RG + RET_V
C_SK = C_SQ + SWA_Q
C_SV = C_SK + SWA_KV
C_SG = C_SV + SWA_KV
C_MR = C_SG + SWA_Q
C_MS = C_MR + D_MODEL
IN_WIDTH = C_MS + D_MODEL

LANES = 128
SUBLANES = 8
MXU_N = 256
PROMPT_TM = 256
WCHUNK = 512
VMEM_LIMIT = 60 * 1024 * 1024


def _dot(a, b):
    return jnp.dot(a, b, preferred_element_type=F32)


def _dot_nt(a, b):
    return lax.dot_general(a, b, (((1,), (1,)), ((), ())), preferred_element_type=F32)


def _dot_tn(a, b):
    return lax.dot_general(a, b, (((0,), (0,)), ((), ())), preferred_element_type=F32)


def _sigmoid(x):
    return 1.0 / (1.0 + jnp.exp(-x))


def _rms_rows(x, g):
    return x * lax.rsqrt(jnp.mean(x * x, axis=-1, keepdims=True) + EPS) * g


def _group_mean_sq(x, gmat):
    x2 = x * x
    hi = x2.astype(BF16)
    lo = (x2 - hi.astype(F32)).astype(BF16)
    return _dot(hi, gmat) + _dot(lo, gmat)


def _rope128(x, cos, sin_signed):
    return x * cos + pltpu.roll(x, 64, 1) * sin_signed


def _rope64(x, cos, sin_signed, first_half):
    rot = jnp.where(first_half, pltpu.roll(x, 96, 1), pltpu.roll(x, 32, 1))
    return x * cos + rot * sin_signed


def _swa_norm_rope(x, gmat, gain, cos, sin_signed, first_half):
    xn = x * lax.rsqrt(_group_mean_sq(x, gmat) + EPS) * gain
    cols = [_rope64(xn[:, c * LANES:(c + 1) * LANES], cos, sin_signed, first_half) for c in range(2)]
    return jnp.concatenate(cols, axis=1)


def _heads_to_rows(qrow):
    row = lax.broadcasted_iota(jnp.int32, (SUBLANES, LANES), 0)
    half = lax.broadcasted_iota(jnp.int32, (SUBLANES, LANES), 1) // SWA_HD
    zeros = jnp.zeros((SUBLANES, LANES), F32)
    tiles = []
    for jt in range(2):
        acc = zeros
        for r in range(8):
            sc = 4 * jt + r // 2
            piece = jnp.broadcast_to(qrow[:, sc * LANES:(sc + 1) * LANES], (SUBLANES, LANES))
            if r % 2 != r // 4:
                piece = pltpu.roll(piece, SWA_HD, 1)
            acc = jnp.where((row == r) & (half == r // 4), piece, acc)
        tiles.append(acc)
    return jnp.concatenate([jnp.concatenate([tiles[0], zeros], axis=1),
                            jnp.concatenate([zeros, tiles[1]], axis=1)], axis=0)


def _rows_to_heads(o):
    low = lax.broadcasted_iota(jnp.int32, (1, LANES), 1) < SWA_HD
    cols = []
    for tc in range(SWA_HEADS // 2):
        pieces = []
        for e in range(2):
            h = 2 * tc + e
            j = h // 4
            piece = o[h:h + 1, (j // 2) * LANES:(j // 2 + 1) * LANES]
            if j % 2 != e:
                piece = pltpu.roll(piece, SWA_HD, 1)
            pieces.append(piece)
        cols.append(jnp.where(low, pieces[0], pieces[1]))
    return jnp.concatenate(cols, axis=1)


def _load_weights_bf16(srcs, dsts, stage, sem):
    jobs = [(src, dst, c0) for src, dst in zip(srcs, dsts) for c0 in range(0, src.shape[1], WCHUNK)]

    def copy(i):
        src, _, c0 = jobs[i]
        return pltpu.make_async_copy(src.at[:, pl.ds(c0, WCHUNK)], stage.at[i % 2], sem.at[i % 2])

    copy(0).start()
    for i, (_, dst, c0) in enumerate(jobs):
        if i + 1 < len(jobs):
            copy(i + 1).start()
        copy(i).wait()
        dst[:, c0:c0 + WCHUNK] = stage[i % 2].astype(BF16)


def _layer_kernel(x_ref, ng_ref, win_hbm, rng_ref, qg_ref, kg_ref, sink_ref, wbr_hbm, wbs_hbm, wo_hbm,
                  rope_ref, dmask_ref, cross_ref, kdec_ref, gmat_ref,
                  xs_hbm, drcos_ref, drsin_ref, dscos_ref, dssin_ref, dst_ref, dkt_ref, dvt_ref,
                  y_ref, st_ref, ko_ref, vo_ref, ys_ref, dsto_ref, dkto_ref, dvto_ref,
                  s_scr, pk_scr, pv_scr, or_scr, os_scr,
                  drq_scr, drk_scr, drv_scr, dsq_scr, dsk_scr, dsv_scr, dor_scr, dos_scr,
                  win_ref, wbr_ref, wbs_ref, wo_ref, wstage, wsem, xs_ref, xsem,
                  *, decay_chunk, decay_step):
    t = pl.program_id(1)
    step = pl.program_id(0) * pl.num_programs(1) + t
    last_step = pl.num_programs(0) * pl.num_programs(1) - 1
    tm = x_ref.shape[0]
    nblk = tm // WINDOW
    nseq = dst_ref.shape[0]

    gmat = gmat_ref[...]
    qg = jnp.concatenate([qg_ref[...]] * (MXU_N // SWA_HD), axis=1)
    kg = jnp.concatenate([kg_ref[...]] * (MXU_N // SWA_HD), axis=1)
    lane = lax.broadcasted_iota(jnp.int32, (1, LANES), 1)
    first_half = (lane % SWA_HD) < (SWA_HD // 2)
    low = lane < SWA_HD

    @pl.when(step == 0)
    def _():
        fetch = pltpu.make_async_copy(xs_hbm.at[:, 0, :], xs_ref, xsem.at[0])
        fetch.start()
        _load_weights_bf16((win_hbm, wbr_hbm, wbs_hbm, wo_hbm), (win_ref, wbr_ref, wbs_ref, wo_ref), wstage, wsem)
        fetch.wait()

    def dproj(lo, hi):
        hs = _rms_rows(xs_ref[...], ng_ref[...]).astype(BF16)
        return _dot(hs, win_ref[:, lo:hi])

    @pl.when(step == 0)
    def _():
        drcos = drcos_ref[...]
        drsin = drsin_ref[...]
        rq = dproj(C_RQ, C_RK)
        rk = dproj(C_RK, C_RV)
        for hd in range(RET_HEADS):
            sl = slice(hd * RET_DK, (hd + 1) * RET_DK)
            drq_scr[:, sl] = _rope128(rq[:, sl], drcos, drsin)
            drk_scr[:, sl] = _rope128(rk[:, sl], drcos, drsin) * (RET_DK ** -0.5)
        drv_scr[...] = dproj(C_RV, C_RG)
        dscos = dscos_ref[...]
        dssin = dssin_ref[...]
        for cb in range(SWA_Q // MXU_N):
            dsq_scr[:, cb * MXU_N:(cb + 1) * MXU_N] = _swa_norm_rope(
                dproj(C_SQ + cb * MXU_N, C_SQ + (cb + 1) * MXU_N), gmat, qg, dscos, dssin,
                first_half) * (SWA_HD ** -0.5)
        dsk_scr[...] = _swa_norm_rope(dproj(C_SK, C_SV), gmat, kg, dscos, dssin, first_half)
        dsv_scr[...] = dproj(C_SV, C_SG)

    @pl.when(t == 0)
    def _():
        s_scr[...] = jnp.zeros_like(s_scr)
        pk_scr[...] = jnp.zeros_like(pk_scr)
        pv_scr[...] = jnp.zeros_like(pv_scr)

    seq0 = step * nseq
    grp = pl.multiple_of((seq0 // SUBLANES) * SUBLANES, SUBLANES)
    rows8 = pl.ds(grp, SUBLANES)
    row8 = lax.broadcasted_iota(jnp.int32, (SUBLANES, 1), 0)
    newest = lane == WINDOW - 1
    dstate = {}

    def column_broadcast(rows_scr):
        rowi = lax.broadcasted_iota(jnp.int32, (SUBLANES, SWA_KV), 0)
        ones_row = lax.broadcasted_iota(jnp.int32, (SUBLANES, nseq * LANES), 0)
        ones_lane = lax.broadcasted_iota(jnp.int32, (SUBLANES, nseq * LANES), 1)
        tile = jnp.zeros((SUBLANES, SWA_KV), F32)
        for s in range(nseq):
            r = rows_scr[pl.ds(seq0 + s, 1), :]
            hi = r.astype(BF16).astype(F32)
            mid = (r - hi).astype(BF16).astype(F32)
            lo = (r - hi) - mid
            for i, term in enumerate((hi, mid, lo)):
                tile = jnp.where(rowi == 3 * s + i, jnp.broadcast_to(term, (SUBLANES, SWA_KV)), tile)
        pick = (ones_row // 3 == ones_lane // LANES) & (ones_row < 3 * nseq)
        return _dot_tn(tile.astype(BF16), jnp.where(pick, 1.0, 0.0).astype(BF16))

    def decode_scores():
        q8 = drq_scr[rows8, :]
        k8 = drk_scr[rows8, :]
        v8 = drv_scr[rows8, :]
        mine = (row8 >= seq0 - grp) & (row8 < seq0 - grp + nseq)
        o_acc = [jnp.zeros((SUBLANES, RET_DV), F32)] * RET_HEADS
        knew = column_broadcast(dsk_scr)
        vnew = column_broadcast(dsv_scr)
        scs, vts = [], []
        for s in range(nseq):
            b = seq0 + s
            sel = row8 == (b - grp)
            for hd in range(RET_HEADS):
                qh = q8[:, hd * RET_DK:(hd + 1) * RET_DK]
                kh = k8[:, hd * RET_DK:(hd + 1) * RET_DK]
                vh = v8[:, hd * RET_DV:(hd + 1) * RET_DV]
                state = dst_ref[s, hd]
                inter = _dot(qh.astype(BF16), state.astype(BF16))
                outer = _dot_tn(jnp.where(sel, kh, 0.0).astype(BF16), vh.astype(BF16))
                dsto_ref[s, hd] = decay_step[hd] * state + outer
                o_acc[hd] = jnp.where(sel, inter, o_acc[hd])
            kt_new = jnp.where(newest, knew[:, s * LANES:(s + 1) * LANES], pltpu.roll(dkt_ref[s], WINDOW - 1, 1))
            vt_new = jnp.where(newest, vnew[:, s * LANES:(s + 1) * LANES], pltpu.roll(dvt_ref[s], WINDOW - 1, 1))
            dkto_ref[s] = kt_new
            dvto_ref[s] = vt_new
            qexp = _heads_to_rows(dsq_scr[pl.ds(b, 1), :]).astype(BF16)
            scs.append(_dot(qexp, kt_new.astype(BF16)))
            vts.append(vt_new.astype(BF16))
        for hd in range(RET_HEADS):
            qh = q8[:, hd * RET_DK:(hd + 1) * RET_DK]
            kh = k8[:, hd * RET_DK:(hd + 1) * RET_DK]
            vh = v8[:, hd * RET_DV:(hd + 1) * RET_DV]
            sl = slice(hd * RET_DV, (hd + 1) * RET_DV)
            o = jnp.sum(qh * kh, axis=-1, keepdims=True) * vh + decay_step[hd] * o_acc[hd]
            dor_scr[rows8, sl] = jnp.where(mine, o, dor_scr[rows8, sl])
        dstate["scs"], dstate["vts"] = scs, vts

    def decode_attend():
        head = lax.broadcasted_iota(jnp.int32, (SWA_HEADS, 1), 0)
        sink_col = jnp.zeros((SWA_HEADS, 1), F32)
        for hh in range(SWA_HEADS):
            sink_col = jnp.where(head == hh, sink_ref[hh], sink_col)
        outs = []
        for s in range(nseq):
            sc = dstate["scs"][s]
            m = jnp.maximum(jnp.max(sc, axis=-1, keepdims=True), sink_col)
            e = jnp.exp(sc - m)
            p = e / (jnp.sum(e, axis=-1, keepdims=True) + jnp.exp(sink_col - m))
            pt = jnp.concatenate([p.astype(BF16), jnp.zeros((LANES - SWA_HEADS, WINDOW), BF16)], axis=0)
            out_t = _dot_nt(dstate["vts"][s], pt)
            outs.append(out_t.T[:SWA_HEADS])
        dstate["outs"] = outs

    def decode_store():
        for s in range(nseq):
            dos_scr[pl.ds(seq0 + s, 1), :] = _rows_to_heads(dstate["outs"][s])

    x = x_ref[...]
    h = _rms_rows(x, ng_ref[...]).astype(BF16)

    def proj(lo, hi):
        return _dot(h, win_ref[:, lo:hi])

    rcos, rsin, scos, ssin = (rope_ref[:, i * LANES:(i + 1) * LANES] for i in range(4))

    rq = proj(C_RQ, C_RK)
    decode_scores()
    rk = proj(C_RK, C_RV)
    rv = proj(C_RV, C_RG).astype(BF16)
    decode_attend()
    k_scale = RET_DK ** -0.5
    scores, inter, vs = [], [], []
    for hd in range(RET_HEADS):
        q = _rope128(rq[:, hd * RET_DK:(hd + 1) * RET_DK], rcos, rsin)
        k = _rope128(rk[:, hd * RET_DK:(hd + 1) * RET_DK], rcos, rsin) * k_scale
        v = rv[:, hd * RET_DV:(hd + 1) * RET_DV]
        state = s_scr[hd]
        scores.append(_dot_nt(q.astype(BF16), k.astype(BF16)))
        inter.append(_dot((q * cross_ref[hd]).astype(BF16), state.astype(BF16)))
        s_scr[hd] = decay_chunk[hd] * state + _dot_tn((k * kdec_ref[hd]).astype(BF16), v)
        vs.append(v)
    decode_store()
    sq_cols = []
    sk = sv = None
    for hd in range(RET_HEADS):
        g = proj(C_RG + hd * RET_DV, C_RG + (hd + 1) * RET_DV)
        sq_cols.append(_swa_norm_rope(proj(C_SQ + hd * MXU_N, C_SQ + (hd + 1) * MXU_N), gmat, qg, scos, ssin,
                                      first_half) * (SWA_HD ** -0.5))
        if hd == 0:
            sk = _swa_norm_rope(proj(C_SK, C_SV), gmat, kg, scos, ssin, first_half)
        if hd == 1:
            sv = proj(C_SV, C_SG)
        o = _dot((scores[hd] * dmask_ref[hd]).astype(BF16), vs[hd]) + inter[hd]
        on = _rms_rows(o, rng_ref[hd:hd + 1, :])
        or_scr[:, hd * RET_DV:(hd + 1) * RET_DV] = (on * (g * _sigmoid(g))).astype(BF16)

    row_i = lax.broadcasted_iota(jnp.int32, (WINDOW, 2 * WINDOW), 0)
    col_j = lax.broadcasted_iota(jnp.int32, (WINDOW, 2 * WINDOW), 1)
    allowed, kbds, vbds = [], {}, {}
    for c in range(nblk):
        lower = row_i + 1
        if c == 0:
            lower = jnp.maximum(lower, jnp.where(t == 0, WINDOW, 0))
        allowed.append((col_j >= lower) & (col_j <= row_i + WINDOW))

    def block_diag(c, j, src, prev_scr):
        r0 = c * WINDOW
        prev = prev_scr[...] if c == 0 else src[r0 - WINDOW:r0]
        cat = jnp.concatenate([prev, src[r0:r0 + WINDOW]], axis=0)
        col, half = divmod(j, 2)
        a = cat[:, col * LANES:(col + 1) * LANES]
        ar = pltpu.roll(a, SWA_HD, 1)
        if half == 0:
            bd = jnp.concatenate([jnp.where(low, a, 0.0), jnp.where(low, 0.0, ar)], axis=0)
        else:
            bd = jnp.concatenate([jnp.where(low, ar, 0.0), jnp.where(low, 0.0, a)], axis=0)
        return bd.astype(BF16)

    quads = [(c, j) for j in range(SWA_KV_HEADS) for c in range(nblk)]

    def qk(i):
        c, j = quads[i]
        kbds[(c, j)] = block_diag(c, j, sk, pk_scr)
        vbds[(c, j)] = block_diag(c, j, sv, pv_scr)
        qp = jnp.concatenate([sq_cols[j][c * WINDOW:(c + 1) * WINDOW, r * LANES:(r + 1) * LANES] for r in range(2)],
                             axis=0).astype(BF16)
        return _dot_nt(qp, kbds[(c, j)])

    chunks = {}

    def filler(i):
        kind, cb = divmod(i, 4)
        if kind == 0:
            chunks["sg", cb] = proj(C_SG + cb * MXU_N, C_SG + (cb + 1) * MXU_N)
        elif kind == 1:
            chunks["mr", cb] = _sigmoid(proj(C_MR + cb * MXU_N, C_MR + (cb + 1) * MXU_N))
        elif kind == 2:
            chunks["ms", cb] = _sigmoid(proj(C_MS + cb * MXU_N, C_MS + (cb + 1) * MXU_N))
        else:
            chunks["br", cb] = _dot(or_scr[...], wbr_ref[:, cb * MXU_N:(cb + 1) * MXU_N])

    order = [0, 4, 8, 12, 1, 5, 9, 13, 2, 6, 10, 14, 3, 7, 11, 15]
    per_unit = len(order) // len(quads)
    s4_next = qk(0)
    for i, (c, j) in enumerate(quads):
        s4 = s4_next
        if i + 1 < len(quads):
            s4_next = qk(i + 1)
        for f in order[i * per_unit:(i + 1) * per_unit]:
            filler(f)
        e_rows, inv_rows = [], []
        for r in range(2):
            es, invs = [], []
            for a in range(2):
                sink = sink_ref[4 * j + 2 * r + a]
                s = jnp.where(allowed[c], s4[r * WINDOW:(r + 1) * WINDOW, a * 2 * WINDOW:(a + 1) * 2 * WINDOW], -jnp.inf)
                m = jnp.maximum(jnp.max(s, axis=-1, keepdims=True), sink)
                e = jnp.exp(s - m)
                invs.append(1.0 / (jnp.sum(e, axis=-1, keepdims=True) + jnp.exp(sink - m)))
                es.append(e.astype(BF16))
            e_rows.append(jnp.concatenate(es, axis=1))
            inv_rows.append(jnp.where(low, invs[0], invs[1]))
        o4 = _dot(jnp.concatenate(e_rows, axis=0), vbds[(c, j)])
        for r in range(2):
            p = 2 * j + r
            o2 = o4[r * WINDOW:(r + 1) * WINDOW] * inv_rows[r]
            g = chunks["sg", j][c * WINDOW:(c + 1) * WINDOW, r * LANES:(r + 1) * LANES]
            os_scr[c * WINDOW:(c + 1) * WINDOW, p * LANES:(p + 1) * LANES] = (o2 * (g * _sigmoid(g))).astype(BF16)
    pk_scr[...] = sk[tm - WINDOW:]
    pv_scr[...] = sv[tm - WINDOW:]

    merged = []
    for cb in range(D_MODEL // MXU_N):
        br_s = _dot(os_scr[...], wbs_ref[:, cb * MXU_N:(cb + 1) * MXU_N])
        merged.append((chunks["mr", cb] * chunks["br", cb] + chunks["ms", cb] * br_s).astype(BF16))
    y_ref[...] = x + _dot(jnp.concatenate(merged, axis=1), wo_ref[...])

    @pl.when(t == pl.num_programs(1) - 1)
    def _():
        st_ref[...] = s_scr[...]
        ko_ref[...] = sk[tm - WINDOW:].T
        vo_ref[...] = sv[tm - WINDOW:].T

    @pl.when(step == last_step)
    def _():
        rg = dproj(C_RG, C_SQ)
        o_r = dor_scr[...]
        parts = []
        for hd in range(RET_HEADS):
            sl = slice(hd * RET_DV, (hd + 1) * RET_DV)
            parts.append(_rms_rows(o_r[:, sl], rng_ref[hd:hd + 1, :]) * (rg[:, sl] * _sigmoid(rg[:, sl])))
        br_r = _dot(jnp.concatenate(parts, axis=1).astype(BF16), wbr_ref[...])
        sg = dproj(C_SG, C_MR)
        br_s = _dot((dos_scr[...] * (sg * _sigmoid(sg))).astype(BF16), wbs_ref[...])
        mrg = _sigmoid(dproj(C_MR, C_MS)) * br_r + _sigmoid(dproj(C_MS, IN_WIDTH)) * br_s
        ys_ref[:, 0, :] = xs_ref[...] + _dot(mrg.astype(BF16), wo_ref[...])


def _const_spec(shape):
    nd = len(shape)
    return pl.BlockSpec(shape, lambda *_: (0,) * nd, pipeline_mode=pl.Buffered(1))


def _rope_tables(pos, d):
    inv = np.float64(ROPE_THETA) ** (-np.arange(0, d, 2, dtype=np.float64) / d)
    ang = pos.astype(np.float64)[:, None] * inv[None, :]
    c, s = np.cos(ang), np.sin(ang)
    reps = LANES // d
    cos = np.tile(np.concatenate([c, c], axis=1), (1, reps))
    sin = np.tile(np.concatenate([-s, s], axis=1), (1, reps))
    return cos.astype(np.float32), sin.astype(np.float32)


def _log_decay():
    return np.log(1.0 - 2.0 ** (-5.0 - np.arange(RET_HEADS, dtype=np.float64)))


def _group_matrix():
    g = np.arange(MXU_N) // SWA_HD
    return jnp.asarray((g[:, None] == g[None, :]).astype(np.float32) / SWA_HD, dtype=BF16)


def _fused_layer(x, xs, state, cache_kt, cache_vt, w, tm):
    B, T, D = x.shape
    nt = T // tm
    nsteps = B * nt
    nb = xs.shape[0]
    assert nb % nsteps == 0 and nb % SUBLANES == 0
    nseq = nb // nsteps
    assert SUBLANES % nseq == 0 and 3 * nseq <= SUBLANES

    lg = _log_decay()
    n = np.arange(tm, dtype=np.float64)
    diff = n[:, None] - n[None, :]
    dmask = np.where(diff[None] >= 0, np.exp(np.maximum(diff, 0.0)[None] * lg[:, None, None]), 0.0).astype(np.float32)
    cross = np.broadcast_to(np.exp((n[None, :] + 1.0) * lg[:, None])[:, :, None], (RET_HEADS, tm, RET_DK))
    kdec = np.broadcast_to(np.exp((tm - 1.0 - n)[None, :] * lg[:, None])[:, :, None], (RET_HEADS, tm, RET_DK))
    decay_chunk = tuple(float(v) for v in np.exp(tm * lg))
    decay_step = tuple(float(v) for v in np.exp(lg))
    pos = np.arange(T)
    rcos, rsin = _rope_tables(pos, RET_DK)
    scos, ssin = _rope_tables(pos, SWA_HD)
    dpos = np.full((1,), PAST_LEN)
    drcos, drsin = _rope_tables(dpos, RET_DK)
    dscos, dssin = _rope_tables(dpos, SWA_HD)

    tok = lambda b, t: (b, t, 0)
    tab = lambda b, t: (t, 0)
    seq4 = lambda b, t: (b * nt + t, 0, 0, 0)
    seq3 = lambda b, t: (b * nt + t, 0, 0)
    in_specs = [
        pl.BlockSpec((None, tm, D), tok),
        _const_spec((1, D)),
        pl.BlockSpec(memory_space=pltpu.HBM),
        _const_spec((RET_HEADS, RET_DV)),
        _const_spec((1, SWA_HD)),
        _const_spec((1, SWA_HD)),
        pl.BlockSpec(memory_space=pltpu.SMEM),
        pl.BlockSpec(memory_space=pltpu.HBM),
        pl.BlockSpec(memory_space=pltpu.HBM),
        pl.BlockSpec(memory_space=pltpu.HBM),
        pl.BlockSpec((tm, 4 * LANES), tab),
        _const_spec((RET_HEADS, tm, tm)),
        _const_spec((RET_HEADS, tm, RET_DK)),
        _const_spec((RET_HEADS, tm, RET_DK)),
        _const_spec((MXU_N, MXU_N)),
        pl.BlockSpec(memory_space=pltpu.HBM),
        _const_spec((1, LANES)),
        _const_spec((1, LANES)),
        _const_spec((1, LANES)),
        _const_spec((1, LANES)),
        pl.BlockSpec((nseq, RET_HEADS, RET_DK, RET_DV), seq4),
        pl.BlockSpec((nseq, SWA_KV, WINDOW), seq3),
        pl.BlockSpec((nseq, SWA_KV, WINDOW), seq3),
    ]
    out_shape = (
        jax.ShapeDtypeStruct((B, T, D), F32),
        jax.ShapeDtypeStruct((B, RET_HEADS, RET_DK, RET_DV), F32),
        jax.ShapeDtypeStruct((B, SWA_KV, WINDOW), F32),
        jax.ShapeDtypeStruct((B, SWA_KV, WINDOW), F32),
        jax.ShapeDtypeStruct((nb, 1, D), F32),
        jax.ShapeDtypeStruct(state.shape, F32),
        jax.ShapeDtypeStruct(cache_kt.shape, F32),
        jax.ShapeDtypeStruct(cache_vt.shape, F32),
    )
    out_specs = (
        pl.BlockSpec((None, tm, D), tok),
        pl.BlockSpec((None, RET_HEADS, RET_DK, RET_DV), lambda b, t: (b, 0, 0, 0)),
        pl.BlockSpec((None, SWA_KV, WINDOW), lambda b, t: (b, 0, 0)),
        pl.BlockSpec((None, SWA_KV, WINDOW), lambda b, t: (b, 0, 0)),
        pl.BlockSpec((nb, 1, D), lambda b, t: (0, 0, 0)),
        pl.BlockSpec((nseq, RET_HEADS, RET_DK, RET_DV), seq4),
        pl.BlockSpec((nseq, SWA_KV, WINDOW), seq3),
        pl.BlockSpec((nseq, SWA_KV, WINDOW), seq3),
    )
    scratch = [
        pltpu.VMEM((RET_HEADS, RET_DK, RET_DV), F32),
        pltpu.VMEM((WINDOW, SWA_KV), F32),
        pltpu.VMEM((WINDOW, SWA_KV), F32),
        pltpu.VMEM((tm, RET_V), BF16),
        pltpu.VMEM((tm, SWA_Q), BF16),
        pltpu.VMEM((nb, RET_QK), F32),
        pltpu.VMEM((nb, RET_QK), F32),
        pltpu.VMEM((nb, RET_V), F32),
        pltpu.VMEM((nb, SWA_Q), F32),
        pltpu.VMEM((nb, SWA_KV), F32),
        pltpu.VMEM((nb, SWA_KV), F32),
        pltpu.VMEM((nb, RET_V), F32),
        pltpu.VMEM((nb, SWA_Q), F32),
        pltpu.VMEM((D, IN_WIDTH), BF16),
        pltpu.VMEM((RET_V, D), BF16),
        pltpu.VMEM((SWA_Q, D), BF16),
        pltpu.VMEM((D, D), BF16),
        pltpu.VMEM((2, D, WCHUNK), F32),
        pltpu.SemaphoreType.DMA((2,)),
        pltpu.VMEM((nb, D), F32),
        pltpu.SemaphoreType.DMA((1,)),
    ]
    return pl.pallas_call(
        functools.partial(_layer_kernel, decay_chunk=decay_chunk, decay_step=decay_step),
        grid=(B, nt),
        in_specs=in_specs,
        out_specs=out_specs,
        out_shape=out_shape,
        scratch_shapes=scratch,
        compiler_params=pltpu.CompilerParams(
            dimension_semantics=("arbitrary", "arbitrary"), vmem_limit_bytes=VMEM_LIMIT),
        name="hybrid_layer",
    )(x, w["ng"], w["win"], w["rng"], w["qg"], w["kg"], w["sinks"], w["wbr"], w["wbs"], w["wo"],
      np.concatenate([rcos, rsin, scos, ssin], axis=1), dmask, cross.astype(np.float32), kdec.astype(np.float32),
      _group_matrix(),
      xs, drcos, drsin, dscos, dssin, state, cache_kt, cache_vt)


def kernel(x_prompt, x_sample, state_ret, cache_swa_k, cache_swa_v, norm_g, w_in, ret_norm_g, swa_q_g, swa_k_g,
           swa_sinks, w_br_ret, w_br_swa, w_out):
    depth = norm_g.shape[0]
    assert depth == 1 and x_sample.shape[1] == 1
    B, T, D = x_prompt.shape
    nb = x_sample.shape[0]
    wc = cache_swa_k.shape[2]
    assert wc == WINDOW and T % PROMPT_TM == 0
    l = 0
    w = {
        "ng": norm_g[l][None, :],
        "win": w_in[l],
        "rng": ret_norm_g[l],
        "qg": swa_q_g[l][None, :],
        "kg": swa_k_g[l][None, :],
        "sinks": swa_sinks[l],
        "wbr": w_br_ret[l],
        "wbs": w_br_swa[l],
        "wo": w_out[l],
    }
    to_t = lambda c: jnp.transpose(c.reshape(nb, wc, SWA_KV), (0, 2, 1))
    from_t = lambda c: jnp.transpose(c, (0, 2, 1)).reshape(1, c.shape[0], wc, SWA_KV_HEADS, SWA_HD)
    yp, rp, kp, vp, ys, rs, ks, vs = _fused_layer(
        x_prompt, x_sample, state_ret[l], to_t(cache_swa_k[l]), to_t(cache_swa_v[l]), w, PROMPT_TM)
    return (yp, ys, rp[None], rs[None],
            from_t(kp), from_t(vp), from_t(ks), from_t(vs))
```

```python
import functools

import numpy as np
import jax
import jax.numpy as jnp
from jax import lax
from jax.experimental import pallas as pl
from jax.experimental.pallas import tpu as pltpu

F32 = jnp.float32
BF16 = jnp.bfloat16

D_MODEL = 1024
RET_HEADS = 4
RET_DK = 128
RET_DV = 256
RET_QK = RET_HEADS * RET_DK
RET_V = RET_HEADS * RET_DV
SWA_HEADS = 16
SWA_KV_HEADS = 4
SWA_HD = 64
SWA_Q = SWA_HEADS * SWA_HD
SWA_KV = SWA_KV_HEADS * SWA_HD
WINDOW = 128
ROPE_THETA = 10000.0
EPS = 1e-6
PAST_LEN = 8192

C_RQ = 0
C_RK = C_RQ + RET_QK
C_RV = C_RK + RET_QK
C_RG = C_RV + RET_V
C_SQ = C_RG + RET_V
C_SK = C_SQ + SWA_Q
C_SV = C_SK + SWA_KV
C_SG = C_SV + SWA_KV
C_MR = C_SG + SWA_Q
C_MS = C_MR + D_MODEL
IN_WIDTH = C_MS + D_MODEL

LANES = 128
SUBLANES = 8
MXU_N = 256
PROMPT_TM = 256
WCHUNK = 512
VMEM_LIMIT = 60 * 1024 * 1024


def _dot(a, b):
    return jnp.dot(a, b, preferred_element_type=F32)


def _dot_nt(a, b):
    return lax.dot_general(a, b, (((1,), (1,)), ((), ())), preferred_element_type=F32)


def _dot_tn(a, b):
    return lax.dot_general(a, b, (((0,), (0,)), ((), ())), preferred_element_type=F32)


def _sigmoid(x):
    return 1.0 / (1.0 + jnp.exp(-x))


def _rms_rows(x, g):
    return x * lax.rsqrt(jnp.mean(x * x, axis=-1, keepdims=True) + EPS) * g


def _group_mean_sq(x, gmat):
    x2 = x * x
    hi = x2.astype(BF16)
    lo = (x2 - hi.astype(F32)).astype(BF16)
    return _dot(hi, gmat) + _dot(lo, gmat)


def _rope128(x, cos, sin_signed):
    return x * cos + pltpu.roll(x, 64, 1) * sin_signed


def _rope64(x, cos, sin_signed, first_half):
    rot = jnp.where(first_half, pltpu.roll(x, 96, 1), pltpu.roll(x, 32, 1))
    return x * cos + rot * sin_signed


def _swa_norm_rope(x, gmat, gain, cos, sin_signed, first_half):
    xn = x * lax.rsqrt(_group_mean_sq(x, gmat) + EPS) * gain
    cols = [_rope64(xn[:, c * LANES:(c + 1) * LANES], cos, sin_signed, first_half) for c in range(2)]
    return jnp.concatenate(cols, axis=1)


def _heads_to_rows(qrow):
    row = lax.broadcasted_iota(jnp.int32, (SUBLANES, LANES), 0)
    half = lax.broadcasted_iota(jnp.int32, (SUBLANES, LANES), 1) // SWA_HD
    zeros = jnp.zeros((SUBLANES, LANES), F32)
    tiles = []
    for jt in range(2):
        acc = zeros
        for r in range(8):
            sc = 4 * jt + r // 2
            piece = jnp.broadcast_to(qrow[:, sc * LANES:(sc + 1) * LANES], (SUBLANES, LANES))
            if r % 2 != r // 4:
                piece = pltpu.roll(piece, SWA_HD, 1)
            acc = jnp.where((row == r) & (half == r // 4), piece, acc)
        tiles.append(acc)
    return jnp.concatenate([jnp.concatenate([tiles[0], zeros], axis=1),
                            jnp.concatenate([zeros, tiles[1]], axis=1)], axis=0)


def _rows_to_heads(o):
    low = lax.broadcasted_iota(jnp.int32, (1, LANES), 1) < SWA_HD
    cols = []
    for tc in range(SWA_HEADS // 2):
        pieces = []
        for e in range(2):
            h = 2 * tc + e
            j = h // 4
            piece = o[h:h + 1, (j // 2) * LANES:(j // 2 + 1) * LANES]
            if j % 2 != e:
                piece = pltpu.roll(piece, SWA_HD, 1)
            pieces.append(piece)
        cols.append(jnp.where(low, pieces[0], pieces[1]))
    return jnp.concatenate(cols, axis=1)


def _load_weights_bf16(srcs, dsts, stage, sem):
    jobs = [(src, dst, c0) for src, dst in zip(srcs, dsts) for c0 in range(0, src.shape[1], WCHUNK)]

    def copy(i):
        src, _, c0 = jobs[i]
        return pltpu.make_async_copy(src.at[:, pl.ds(c0, WCHUNK)], stage.at[i % 2], sem.at[i % 2])

    copy(0).start()
    for i, (_, dst, c0) in enumerate(jobs):
        if i + 1 < len(jobs):
            copy(i + 1).start()
        copy(i).wait()
        dst[:, c0:c0 + WCHUNK] = stage[i % 2].astype(BF16)


RESIDENT = ("ng_ref", "win_hbm", "rng_ref", "qg_ref", "kg_ref", "sink_ref", "wbr_hbm", "wbs_hbm", "wo_hbm",
            "dmask_ref", "cross_ref", "kdec_ref", "gmat_ref", "xs_hbm", "drcos_ref", "drsin_ref", "dscos_ref", "dssin_ref")
SCRATCH = ("s_scr", "pk_scr", "pv_scr", "or_scr", "os_scr",
           "drq_scr", "drk_scr", "drv_scr", "dsq_scr", "dsk_scr", "dsv_scr", "dor_scr", "dos_scr",
           "win_ref", "wbr_ref", "wbs_ref", "wo_ref", "wstage", "wsem", "xs_ref", "xsem")


def _layer_kernel(x_hbm, rope_hbm, dst_hbm, dkt_hbm, dvt_hbm, *rest, grid, in_specs, out_specs, decay_chunk, decay_step):
    resident = dict(zip(RESIDENT, rest))
    y_hbm, st_hbm, ko_hbm, vo_hbm, ys_ref, dsto_hbm, dkto_hbm, dvto_hbm = rest[len(RESIDENT):len(RESIDENT) + 8]
    scratch = dict(zip(SCRATCH, rest[len(RESIDENT) + 8:]))
    step = functools.partial(_layer_step, ys_ref=ys_ref, decay_chunk=decay_chunk, decay_step=decay_step,
                             **resident, **scratch)
    pltpu.emit_pipeline(step, grid=grid, in_specs=in_specs, out_specs=out_specs)(
        x_hbm, rope_hbm, dst_hbm, dkt_hbm, dvt_hbm, y_hbm, st_hbm, ko_hbm, vo_hbm, dsto_hbm, dkto_hbm, dvto_hbm)


def _layer_step(x_ref, rope_ref, dst_ref, dkt_ref, dvt_ref, y_ref, st_ref, ko_ref, vo_ref, dsto_ref, dkto_ref, dvto_ref,
                *, ng_ref, win_hbm, rng_ref, qg_ref, kg_ref, sink_ref, wbr_hbm, wbs_hbm, wo_hbm,
                dmask_ref, cross_ref, kdec_ref, gmat_ref, xs_hbm, drcos_ref, drsin_ref, dscos_ref, dssin_ref, ys_ref,
                s_scr, pk_scr, pv_scr, or_scr, os_scr,
                drq_scr, drk_scr, drv_scr, dsq_scr, dsk_scr, dsv_scr, dor_scr, dos_scr,
                win_ref, wbr_ref, wbs_ref, wo_ref, wstage, wsem, xs_ref, xsem, decay_chunk, decay_step):
    t = pl.program_id(1)
    step = pl.program_id(0) * pl.num_programs(1) + t
    last_step = pl.num_programs(0) * pl.num_programs(1) - 1
    tm = x_ref.shape[0]
    nblk = tm // WINDOW
    nseq = dst_ref.shape[0]

    gmat = gmat_ref[...]
    qg = jnp.concatenate([qg_ref[...]] * (MXU_N // SWA_HD), axis=1)
    kg = jnp.concatenate([kg_ref[...]] * (MXU_N // SWA_HD), axis=1)
    lane = lax.broadcasted_iota(jnp.int32, (1, LANES), 1)
    first_half = (lane % SWA_HD) < (SWA_HD // 2)
    low = lane < SWA_HD

    @pl.when(step == 0)
    def _():
        fetch = pltpu.make_async_copy(xs_hbm.at[:, 0, :], xs_ref, xsem.at[0])
        fetch.start()
        _load_weights_bf16((win_hbm, wbr_hbm, wbs_hbm, wo_hbm), (win_ref, wbr_ref, wbs_ref, wo_ref), wstage, wsem)
        fetch.wait()

    def dproj(lo, hi):
        hs = _rms_rows(xs_ref[...], ng_ref[...]).astype(BF16)
        return _dot(hs, win_ref[:, lo:hi])

    @pl.when(step == 0)
    def _():
        drcos = drcos_ref[...]
        drsin = drsin_ref[...]
        rq = dproj(C_RQ, C_RK)
        rk = dproj(C_RK, C_RV)
        for hd in range(RET_HEADS):
            sl = slice(hd * RET_DK, (hd + 1) * RET_DK)
            drq_scr[:, sl] = _rope128(rq[:, sl], drcos, drsin)
            drk_scr[:, sl] = _rope128(rk[:, sl], drcos, drsin) * (RET_DK ** -0.5)
        drv_scr[...] = dproj(C_RV, C_RG)
        dscos = dscos_ref[...]
        dssin = dssin_ref[...]
        for cb in range(SWA_Q // MXU_N):
            dsq_scr[:, cb * MXU_N:(cb + 1) * MXU_N] = _swa_norm_rope(
                dproj(C_SQ + cb * MXU_N, C_SQ + (cb + 1) * MXU_N), gmat, qg, dscos, dssin,
                first_half) * (SWA_HD ** -0.5)
        dsk_scr[...] = _swa_norm_rope(dproj(C_SK, C_SV), gmat, kg, dscos, dssin, first_half)
        dsv_scr[...] = dproj(C_SV, C_SG)

    @pl.when(t == 0)
    def _():
        s_scr[...] = jnp.zeros_like(s_scr)
        pk_scr[...] = jnp.zeros_like(pk_scr)
        pv_scr[...] = jnp.zeros_like(pv_scr)

    seq0 = step * nseq
    grp = pl.multiple_of((seq0 // SUBLANES) * SUBLANES, SUBLANES)
    rows8 = pl.ds(grp, SUBLANES)
    row8 = lax.broadcasted_iota(jnp.int32, (SUBLANES, 1), 0)
    newest = lane == WINDOW - 1
    dstate = {}

    def column_broadcast(rows_scr):
        rowi = lax.broadcasted_iota(jnp.int32, (SUBLANES, SWA_KV), 0)
        ones_row = lax.broadcasted_iota(jnp.int32, (SUBLANES, nseq * LANES), 0)
        ones_lane = lax.broadcasted_iota(jnp.int32, (SUBLANES, nseq * LANES), 1)
        tile = jnp.zeros((SUBLANES, SWA_KV), F32)
        for s in range(nseq):
            r = rows_scr[pl.ds(seq0 + s, 1), :]
            hi = r.astype(BF16).astype(F32)
            mid = (r - hi).astype(BF16).astype(F32)
            lo = (r - hi) - mid
            for i, term in enumerate((hi, mid, lo)):
                tile = jnp.where(rowi == 3 * s + i, jnp.broadcast_to(term, (SUBLANES, SWA_KV)), tile)
        pick = (ones_row // 3 == ones_lane // LANES) & (ones_row < 3 * nseq)
        return _dot_tn(tile.astype(BF16), jnp.where(pick, 1.0, 0.0).astype(BF16))

    def decode_scores():
        q8 = drq_scr[rows8, :]
        k8 = drk_scr[rows8, :]
        v8 = drv_scr[rows8, :]
        mine = (row8 >= seq0 - grp) & (row8 < seq0 - grp + nseq)
        o_acc = [jnp.zeros((SUBLANES, RET_DV), F32)] * RET_HEADS
        knew = column_broadcast(dsk_scr)
        vnew = column_broadcast(dsv_scr)
        scs, vts = [], []
        for s in range(nseq):
            b = seq0 + s
            sel = row8 == (b - grp)
            for hd in range(RET_HEADS):
                qh = q8[:, hd * RET_DK:(hd + 1) * RET_DK]
                kh = k8[:, hd * RET_DK:(hd + 1) * RET_DK]
                vh = v8[:, hd * RET_DV:(hd + 1) * RET_DV]
                state = dst_ref[s, hd]
                inter = _dot(qh.astype(BF16), state.astype(BF16))
                outer = _dot_tn(jnp.where(sel, kh, 0.0).astype(BF16), vh.astype(BF16))
                dsto_ref[s, hd] = decay_step[hd] * state + outer
                o_acc[hd] = jnp.where(sel, inter, o_acc[hd])
            kt_new = jnp.where(newest, knew[:, s * LANES:(s + 1) * LANES], pltpu.roll(dkt_ref[s], WINDOW - 1, 1))
            vt_new = jnp.where(newest, vnew[:, s * LANES:(s + 1) * LANES], pltpu.roll(dvt_ref[s], WINDOW - 1, 1))
            dkto_ref[s] = kt_new
            dvto_ref[s] = vt_new
            qexp = _heads_to_rows(dsq_scr[pl.ds(b, 1), :]).astype(BF16)
            scs.append(_dot(qexp, kt_new.astype(BF16)))
            vts.append(vt_new.astype(BF16))
        for hd in range(RET_HEADS):
            qh = q8[:, hd * RET_DK:(hd + 1) * RET_DK]
            kh = k8[:, hd * RET_DK:(hd + 1) * RET_DK]
            vh = v8[:, hd * RET_DV:(hd + 1) * RET_DV]
            sl = slice(hd * RET_DV, (hd + 1) * RET_DV)
            o = jnp.sum(qh * kh, axis=-1, keepdims=True) * vh + decay_step[hd] * o_acc[hd]
            dor_scr[rows8, sl] = jnp.where(mine, o, dor_scr[rows8, sl])
        dstate["scs"], dstate["vts"] = scs, vts

    def decode_attend():
        head = lax.broadcasted_iota(jnp.int32, (SWA_HEADS, 1), 0)
        sink_col = jnp.zeros((SWA_HEADS, 1), F32)
        for hh in range(SWA_HEADS):
            sink_col = jnp.where(head == hh, sink_ref[hh], sink_col)
        outs = []
        for s in range(nseq):
            sc = dstate["scs"][s]
            m = jnp.maximum(jnp.max(sc, axis=-1, keepdims=True), sink_col)
            e = jnp.exp(sc - m)
            p = e / (jnp.sum(e, axis=-1, keepdims=True) + jnp.exp(sink_col - m))
            pt = jnp.concatenate([p.astype(BF16), jnp.zeros((LANES - SWA_HEADS, WINDOW), BF16)], axis=0)
            out_t = _dot_nt(dstate["vts"][s], pt)
            outs.append(out_t.T[:SWA_HEADS])
        dstate["outs"] = outs

    def decode_store():
        for s in range(nseq):
            dos_scr[pl.ds(seq0 + s, 1), :] = _rows_to_heads(dstate["outs"][s])

    x = x_ref[...]
    h = _rms_rows(x, ng_ref[...]).astype(BF16)

    def proj(lo, hi):
        return _dot(h, win_ref[:, lo:hi])

    rcos, rsin, scos, ssin = (rope_ref[:, i * LANES:(i + 1) * LANES] for i in range(4))

    rq = proj(C_RQ, C_RK)
    decode_scores()
    rk = proj(C_RK, C_RV)
    rv = proj(C_RV, C_RG).astype(BF16)
    decode_attend()
    k_scale = RET_DK ** -0.5
    scores, inter, vs = [], [], []
    for hd in range(RET_HEADS):
        q = _rope128(rq[:, hd * RET_DK:(hd + 1) * RET_DK], rcos, rsin)
        k = _rope128(rk[:, hd * RET_DK:(hd + 1) * RET_DK], rcos, rsin) * k_scale
        v = rv[:, hd * RET_DV:(hd + 1) * RET_DV]
        state = s_scr[hd]
        scores.append(_dot_nt(q.astype(BF16), k.astype(BF16)))
        inter.append(_dot((q * cross_ref[hd]).astype(BF16), state.astype(BF16)))
        s_scr[hd] = decay_chunk[hd] * state + _dot_tn((k * kdec_ref[hd]).astype(BF16), v)
        vs.append(v)
    decode_store()
    sq_cols = []
    sk = sv = None
    for hd in range(RET_HEADS):
        g = proj(C_RG + hd * RET_DV, C_RG + (hd + 1) * RET_DV)
        sq_cols.append(_swa_norm_rope(proj(C_SQ + hd * MXU_N, C_SQ + (hd + 1) * MXU_N), gmat, qg, scos, ssin,
                                      first_half) * (SWA_HD ** -0.5))
        if hd == 0:
            sk = _swa_norm_rope(proj(C_SK, C_SV), gmat, kg, scos, ssin, first_half)
        if hd == 1:
            sv = proj(C_SV, C_SG)
        o = _dot((scores[hd] * dmask_ref[hd]).astype(BF16), vs[hd]) + inter[hd]
        on = _rms_rows(o, rng_ref[hd:hd + 1, :])
        or_scr[:, hd * RET_DV:(hd + 1) * RET_DV] = (on * (g * _sigmoid(g))).astype(BF16)

    row_i = lax.broadcasted_iota(jnp.int32, (WINDOW, 2 * WINDOW), 0)
    col_j = lax.broadcasted_iota(jnp.int32, (WINDOW, 2 * WINDOW), 1)
    allowed, kbds, vbds = [], {}, {}
    for c in range(nblk):
        lower = row_i + 1
        if c == 0:
            lower = jnp.maximum(lower, jnp.where(t == 0, WINDOW, 0))
        allowed.append((col_j >= lower) & (col_j <= row_i + WINDOW))

    def block_diag(c, j, src, prev_scr):
        r0 = c * WINDOW
        prev = prev_scr[...] if c == 0 else src[r0 - WINDOW:r0]
        cat = jnp.concatenate([prev, src[r0:r0 + WINDOW]], axis=0)
        col, half = divmod(j, 2)
        a = cat[:, col * LANES:(col + 1) * LANES]
        ar = pltpu.roll(a, SWA_HD, 1)
        if half == 0:
            bd = jnp.concatenate([jnp.where(low, a, 0.0), jnp.where(low, 0.0, ar)], axis=0)
        else:
            bd = jnp.concatenate([jnp.where(low, ar, 0.0), jnp.where(low, 0.0, a)], axis=0)
        return bd.astype(BF16)

    quads = [(c, j) for j in range(SWA_KV_HEADS) for c in range(nblk)]

    def qk(i):
        c, j = quads[i]
        kbds[(c, j)] = block_diag(c, j, sk, pk_scr)
        vbds[(c, j)] = block_diag(c, j, sv, pv_scr)
        qp = jnp.concatenate([sq_cols[j][c * WINDOW:(c + 1) * WINDOW, r * LANES:(r + 1) * LANES] for r in range(2)],
                             axis=0).astype(BF16)
        return _dot_nt(qp, kbds[(c, j)])

    chunks = {}

    def filler(i):
        kind, cb = divmod(i, 4)
        if kind == 0:
            chunks["sg", cb] = proj(C_SG + cb * MXU_N, C_SG + (cb + 1) * MXU_N)
        elif kind == 1:
            chunks["mr", cb] = _sigmoid(proj(C_MR + cb * MXU_N, C_MR + (cb + 1) * MXU_N))
        elif kind == 2:
            chunks["ms", cb] = _sigmoid(proj(C_MS + cb * MXU_N, C_MS + (cb + 1) * MXU_N))
        else:
            chunks["br", cb] = _dot(or_scr[...], wbr_ref[:, cb * MXU_N:(cb + 1) * MXU_N])

    order = [0, 4, 8, 12, 1, 5, 9, 13, 2, 6, 10, 14, 3, 7, 11, 15]
    per_unit = len(order) // len(quads)
    s4_next = qk(0)
    for i, (c, j) in enumerate(quads):
        s4 = s4_next
        if i + 1 < len(quads):
            s4_next = qk(i + 1)
        for f in order[i * per_unit:(i + 1) * per_unit]:
            filler(f)
        e_rows, inv_rows = [], []
        for r in range(2):
            es, invs = [], []
            for a in range(2):
                sink = sink_ref[4 * j + 2 * r + a]
                s = jnp.where(allowed[c], s4[r * WINDOW:(r + 1) * WINDOW, a * 2 * WINDOW:(a + 1) * 2 * WINDOW], -jnp.inf)
                m = jnp.maximum(jnp.max(s, axis=-1, keepdims=True), sink)
                e = jnp.exp(s - m)
                invs.append(1.0 / (jnp.sum(e, axis=-1, keepdims=True) + jnp.exp(sink - m)))
                es.append(e.astype(BF16))
            e_rows.append(jnp.concatenate(es, axis=1))
            inv_rows.append(jnp.where(low, invs[0], invs[1]))
        o4 = _dot(jnp.concatenate(e_rows, axis=0), vbds[(c, j)])
        for r in range(2):
            p = 2 * j + r
            o2 = o4[r * WINDOW:(r + 1) * WINDOW] * inv_rows[r]
            g = chunks["sg", j][c * WINDOW:(c + 1) * WINDOW, r * LANES:(r + 1) * LANES]
            os_scr[c * WINDOW:(c + 1) * WINDOW, p * LANES:(p + 1) * LANES] = (o2 * (g * _sigmoid(g))).astype(BF16)
    pk_scr[...] = sk[tm - WINDOW:]
    pv_scr[...] = sv[tm - WINDOW:]

    merged = []
    for cb in range(D_MODEL // MXU_N):
        br_s = _dot(os_scr[...], wbs_ref[:, cb * MXU_N:(cb + 1) * MXU_N])
        merged.append((chunks["mr", cb] * chunks["br", cb] + chunks["ms", cb] * br_s).astype(BF16))
    y_ref[...] = x + _dot(jnp.concatenate(merged, axis=1), wo_ref[...])

    @pl.when(t == pl.num_programs(1) - 1)
    def _():
        st_ref[...] = s_scr[...]
        ko_ref[...] = sk[tm - WINDOW:].T
        vo_ref[...] = sv[tm - WINDOW:].T

    @pl.when(step == last_step)
    def _():
        rg = dproj(C_RG, C_SQ)
        o_r = dor_scr[...]
        parts = []
        for hd in range(RET_HEADS):
            sl = slice(hd * RET_DV, (hd + 1) * RET_DV)
            parts.append(_rms_rows(o_r[:, sl], rng_ref[hd:hd + 1, :]) * (rg[:, sl] * _sigmoid(rg[:, sl])))
        br_r = _dot(jnp.concatenate(parts, axis=1).astype(BF16), wbr_ref[...])
        sg = dproj(C_SG, C_MR)
        br_s = _dot((dos_scr[...] * (sg * _sigmoid(sg))).astype(BF16), wbs_ref[...])
        mrg = _sigmoid(dproj(C_MR, C_MS)) * br_r + _sigmoid(dproj(C_MS, IN_WIDTH)) * br_s
        ys_ref[:, 0, :] = xs_ref[...] + _dot(mrg.astype(BF16), wo_ref[...])


def _rope_tables(pos, d):
    inv = np.float64(ROPE_THETA) ** (-np.arange(0, d, 2, dtype=np.float64) / d)
    ang = pos.astype(np.float64)[:, None] * inv[None, :]
    c, s = np.cos(ang), np.sin(ang)
    reps = LANES // d
    cos = np.tile(np.concatenate([c, c], axis=1), (1, reps))
    sin = np.tile(np.concatenate([-s, s], axis=1), (1, reps))
    return cos.astype(np.float32), sin.astype(np.float32)


def _log_decay():
    return np.log(1.0 - 2.0 ** (-5.0 - np.arange(RET_HEADS, dtype=np.float64)))


def _group_matrix():
    g = np.arange(MXU_N) // SWA_HD
    return jnp.asarray((g[:, None] == g[None, :]).astype(np.float32) / SWA_HD, dtype=BF16)


def _fused_layer(x, xs, state, cache_kt, cache_vt, w, tm):
    B, T, D = x.shape
    nt = T // tm
    nsteps = B * nt
    nb = xs.shape[0]
    assert nb % nsteps == 0 and nb % SUBLANES == 0
    nseq = nb // nsteps
    assert SUBLANES % nseq == 0 and 3 * nseq <= SUBLANES

    lg = _log_decay()
    n = np.arange(tm, dtype=np.float64)
    diff = n[:, None] - n[None, :]
    dmask = np.where(diff[None] >= 0, np.exp(np.maximum(diff, 0.0)[None] * lg[:, None, None]), 0.0).astype(np.float32)
    cross = np.broadcast_to(np.exp((n[None, :] + 1.0) * lg[:, None])[:, :, None], (RET_HEADS, tm, RET_DK))
    kdec = np.broadcast_to(np.exp((tm - 1.0 - n)[None, :] * lg[:, None])[:, :, None], (RET_HEADS, tm, RET_DK))
    decay_chunk = tuple(float(v) for v in np.exp(tm * lg))
    decay_step = tuple(float(v) for v in np.exp(lg))
    pos = np.arange(T)
    rcos, rsin = _rope_tables(pos, RET_DK)
    scos, ssin = _rope_tables(pos, SWA_HD)
    dpos = np.full((1,), PAST_LEN)
    drcos, drsin = _rope_tables(dpos, RET_DK)
    dscos, dssin = _rope_tables(dpos, SWA_HD)

    tok = lambda b, t: (b, t, 0)
    tab = lambda b, t: (t, 0)
    seq4 = lambda b, t: (b * nt + t, 0, 0, 0)
    seq3 = lambda b, t: (b * nt + t, 0, 0)
    step_in_specs = [
        pl.BlockSpec((None, tm, D), tok),
        pl.BlockSpec((tm, 4 * LANES), tab),
        pl.BlockSpec((nseq, RET_HEADS, RET_DK, RET_DV), seq4),
        pl.BlockSpec((nseq, SWA_KV, WINDOW), seq3),
        pl.BlockSpec((nseq, SWA_KV, WINDOW), seq3),
    ]
    step_out_specs = [
        pl.BlockSpec((None, tm, D), tok),
        pl.BlockSpec((None, RET_HEADS, RET_DK, RET_DV), lambda b, t: (b, 0, 0, 0)),
        pl.BlockSpec((None, SWA_KV, WINDOW), lambda b, t: (b, 0, 0)),
        pl.BlockSpec((None, SWA_KV, WINDOW), lambda b, t: (b, 0, 0)),
        pl.BlockSpec((nseq, RET_HEADS, RET_DK, RET_DV), seq4),
        pl.BlockSpec((nseq, SWA_KV, WINDOW), seq3),
        pl.BlockSpec((nseq, SWA_KV, WINDOW), seq3),
    ]
    hbm = pl.BlockSpec(memory_space=pltpu.HBM)
    vmem = pl.BlockSpec(memory_space=pltpu.VMEM)
    in_specs = [hbm] * 5 + [vmem, hbm, vmem, vmem, vmem, pl.BlockSpec(memory_space=pltpu.SMEM), hbm, hbm, hbm,
                            vmem, vmem, vmem, vmem, hbm, vmem, vmem, vmem, vmem]
    assert len(in_specs) == 5 + len(RESIDENT)
    out_shape = (
        jax.ShapeDtypeStruct((B, T, D), F32),
        jax.ShapeDtypeStruct((B, RET_HEADS, RET_DK, RET_DV), F32),
        jax.ShapeDtypeStruct((B, SWA_KV, WINDOW), F32),
        jax.ShapeDtypeStruct((B, SWA_KV, WINDOW), F32),
        jax.ShapeDtypeStruct((nb, 1, D), F32),
        jax.ShapeDtypeStruct(state.shape, F32),
        jax.ShapeDtypeStruct(cache_kt.shape, F32),
        jax.ShapeDtypeStruct(cache_vt.shape, F32),
    )
    out_specs = (hbm, hbm, hbm, hbm, vmem, hbm, hbm, hbm)
    scratch = [
        pltpu.VMEM((RET_HEADS, RET_DK, RET_DV), F32),
        pltpu.VMEM((WINDOW, SWA_KV), F32),
        pltpu.VMEM((WINDOW, SWA_KV), F32),
        pltpu.VMEM((tm, RET_V), BF16),
        pltpu.VMEM((tm, SWA_Q), BF16),
        pltpu.VMEM((nb, RET_QK), F32),
        pltpu.VMEM((nb, RET_QK), F32),
        pltpu.VMEM((nb, RET_V), F32),
        pltpu.VMEM((nb, SWA_Q), F32),
        pltpu.VMEM((nb, SWA_KV), F32),
        pltpu.VMEM((nb, SWA_KV), F32),
        pltpu.VMEM((nb, RET_V), F32),
        pltpu.VMEM((nb, SWA_Q), F32),
        pltpu.VMEM((D, IN_WIDTH), BF16),
        pltpu.VMEM((RET_V, D), BF16),
        pltpu.VMEM((SWA_Q, D), BF16),
        pltpu.VMEM((D, D), BF16),
        pltpu.VMEM((2, D, WCHUNK), F32),
        pltpu.SemaphoreType.DMA((2,)),
        pltpu.VMEM((nb, D), F32),
        pltpu.SemaphoreType.DMA((1,)),
    ]
    return pl.pallas_call(
        functools.partial(_layer_kernel, grid=(B, nt), in_specs=step_in_specs, out_specs=step_out_specs,
                          decay_chunk=decay_chunk, decay_step=decay_step),
        in_specs=in_specs,
        out_specs=out_specs,
        out_shape=out_shape,
        scratch_shapes=scratch,
        compiler_params=pltpu.CompilerParams(vmem_limit_bytes=VMEM_LIMIT),
        name="hybrid_layer",
    )(x, np.concatenate([rcos, rsin, scos, ssin], axis=1), state, cache_kt, cache_vt,
      w["ng"], w["win"], w["rng"], w["qg"], w["kg"], w["sinks"], w["wbr"], w["wbs"], w["wo"],
      dmask, cross.astype(np.float32), kdec.astype(np.float32), _group_matrix(),
      xs, drcos, drsin, dscos, dssin)


def kernel(x_prompt, x_sample, state_ret, cache_swa_k, cache_swa_v, norm_g, w_in, ret_norm_g, swa_q_g, swa_k_g,
           swa_sinks, w_br_ret, w_br_swa, w_out):
    depth = norm_g.shape[0]
    assert depth == 1 and x_sample.shape[1] == 1
    B, T, D = x_prompt.shape
    nb = x_sample.shape[0]
    wc = cache_swa_k.shape[2]
    assert wc == WINDOW and T % PROMPT_TM == 0
    l = 0
    w = {
        "ng": norm_g[l][None, :],
        "win": w_in[l],
        "rng": ret_norm_g[l],
        "qg": swa_q_g[l][None, :],
        "kg": swa_k_g[l][None, :],
        "sinks": swa_sinks[l],
        "wbr": w_br_ret[l],
        "wbs": w_br_swa[l],
        "wo": w_out[l],
    }
    to_t = lambda c: jnp.transpose(c.reshape(nb, wc, SWA_KV), (0, 2, 1))
    from_t = lambda c: jnp.transpose(c, (0, 2, 1)).reshape(1, c.shape[0], wc, SWA_KV_HEADS, SWA_HD)
    yp, rp, kp, vp, ys, rs, ks, vs = _fused_layer(
        x_prompt, x_sample, state_ret[l], to_t(cache_swa_k[l]), to_t(cache_swa_v[l]), w, PROMPT_TM)
    return (yp, ys, rp[None], rs[None],
            from_t(kp), from_t(vp), from_t(ks), from_t(vs))
```

```python
import functools

import numpy as np
import jax
import jax.numpy as jnp
from jax import lax
from jax.experimental import pallas as pl
from jax.experimental.pallas import tpu as pltpu

F32 = jnp.float32
BF16 = jnp.bfloat16

D_MODEL = 1024
RET_HEADS = 4
RET_DK = 128
RET_DV = 256
RET_QK = RET_HEADS * RET_DK
RET_V = RET_HEADS * RET_DV
SWA_HEADS = 16
SWA_KV_HEADS = 4
SWA_HD = 64
SWA_Q = SWA_HEADS * SWA_HD
SWA_KV = SWA_KV_HEADS * SWA_HD
WINDOW = 128
ROPE_THETA = 10000.0
EPS = 1e-6
PAST_LEN = 8192

C_RQ = 0
C_RK = C_RQ + RET_QK
C_RV = C_RK + RET_QK
C_RG = C_RV + RET_V
C_SQ = C_RG + RET_V
C_SK = C_SQ + SWA_Q
C_SV = C_SK + SWA_KV
C_SG = C_SV + SWA_KV
C_MR = C_SG + SWA_Q
C_MS = C_MR + D_MODEL
IN_WIDTH = C_MS + D_MODEL

LANES = 128
SUBLANES = 8
MXU_N = 256
PROMPT_TM = 256
WCHUNK = 512
VMEM_LIMIT = 60 * 1024 * 1024


def _dot(a, b):
    return jnp.dot(a, b, preferred_element_type=F32)


def _dot_nt(a, b):
    return lax.dot_general(a, b, (((1,), (1,)), ((), ())), preferred_element_type=F32)


def _dot_tn(a, b):
    return lax.dot_general(a, b, (((0,), (0,)), ((), ())), preferred_element_type=F32)


def _sigmoid(x):
    return 1.0 / (1.0 + jnp.exp(-x))


def _rms_rows(x, g):
    return x * lax.rsqrt(jnp.mean(x * x, axis=-1, keepdims=True) + EPS) * g


def _group_mean_sq(x, gmat):
    x2 = x * x
    hi = x2.astype(BF16)
    lo = (x2 - hi.astype(F32)).astype(BF16)
    return _dot(hi, gmat) + _dot(lo, gmat)


def _rope128(x, cos, sin_signed):
    return x * cos + pltpu.roll(x, 64, 1) * sin_signed


def _rope64(x, cos, sin_signed, first_half):
    rot = jnp.where(first_half, pltpu.roll(x, 96, 1), pltpu.roll(x, 32, 1))
    return x * cos + rot * sin_signed


def _swa_norm_rope(x, gmat, gain, cos, sin_signed, first_half):
    xn = x * lax.rsqrt(_group_mean_sq(x, gmat) + EPS) * gain
    cols = [_rope64(xn[:, c * LANES:(c + 1) * LANES], cos, sin_signed, first_half) for c in range(2)]
    return jnp.concatenate(cols, axis=1)


def _heads_to_rows(qrow):
    row = lax.broadcasted_iota(jnp.int32, (SUBLANES, LANES), 0)
    half = lax.broadcasted_iota(jnp.int32, (SUBLANES, LANES), 1) // SWA_HD
    zeros = jnp.zeros((SUBLANES, LANES), F32)
    tiles = []
    for jt in range(2):
        acc = zeros
        for r in range(8):
            sc = 4 * jt + r // 2
            piece = jnp.broadcast_to(qrow[:, sc * LANES:(sc + 1) * LANES], (SUBLANES, LANES))
            if r % 2 != r // 4:
                piece = pltpu.roll(piece, SWA_HD, 1)
            acc = jnp.where((row == r) & (half == r // 4), piece, acc)
        tiles.append(acc)
    return jnp.concatenate([jnp.concatenate([tiles[0], zeros], axis=1),
                            jnp.concatenate([zeros, tiles[1]], axis=1)], axis=0)


def _rows_to_heads(o):
    low = lax.broadcasted_iota(jnp.int32, (1, LANES), 1) < SWA_HD
    cols = []
    for tc in range(SWA_HEADS // 2):
        pieces = []
        for e in range(2):
            h = 2 * tc + e
            j = h // 4
            piece = o[h:h + 1, (j // 2) * LANES:(j // 2 + 1) * LANES]
            if j % 2 != e:
                piece = pltpu.roll(piece, SWA_HD, 1)
            pieces.append(piece)
        cols.append(jnp.where(low, pieces[0], pieces[1]))
    return jnp.concatenate(cols, axis=1)


def _load_weights_bf16(srcs, dsts, stage, sem):
    jobs = [(src, dst, c0) for src, dst in zip(srcs, dsts) for c0 in range(0, src.shape[1], WCHUNK)]

    def copy(i):
        src, _, c0 = jobs[i]
        return pltpu.make_async_copy(src.at[:, pl.ds(c0, WCHUNK)], stage.at[i % 2], sem.at[i % 2])

    copy(0).start()
    for i, (_, dst, c0) in enumerate(jobs):
        if i + 1 < len(jobs):
            copy(i + 1).start()
        copy(i).wait()
        dst[:, c0:c0 + WCHUNK] = stage[i % 2].astype(BF16)


RESIDENT = ("ng_ref", "win_hbm", "rng_ref", "qg_ref", "kg_ref", "sink_ref", "wbr_hbm", "wbs_hbm", "wo_hbm",
            "dmask_ref", "cross_ref", "kdec_ref", "gmat_ref", "xs_hbm", "drcos_ref", "drsin_ref", "dscos_ref", "dssin_ref")
SCRATCH = ("pk_scr", "pv_scr", "or_scr", "os_scr",
           "drq_scr", "drk_scr", "drv_scr", "dsq_scr", "dsk_scr", "dsv_scr", "dor_scr", "dos_scr",
           "win_ref", "wbr_ref", "wbs_ref", "wo_ref", "wstage", "wsem", "xs_ref", "xsem")


def _layer_kernel(x_hbm, rope_hbm, dst_hbm, dkt_hbm, dvt_hbm, *rest, grid, in_specs, out_specs, decay_chunk, decay_step):
    resident = dict(zip(RESIDENT, rest))
    y_hbm, st_hbm, ko_hbm, vo_hbm, ys_ref, dsto_hbm, dkto_hbm, dvto_hbm = rest[len(RESIDENT):len(RESIDENT) + 8]
    scratch = dict(zip(SCRATCH, rest[len(RESIDENT) + 8:]))
    step = functools.partial(_layer_step, ys_ref=ys_ref, decay_chunk=decay_chunk, decay_step=decay_step,
                             **resident, **scratch)
    pltpu.emit_pipeline(step, grid=grid, in_specs=in_specs, out_specs=out_specs)(
        x_hbm, rope_hbm, dst_hbm, dkt_hbm, dvt_hbm, y_hbm, st_hbm, ko_hbm, vo_hbm, dsto_hbm, dkto_hbm, dvto_hbm)


def _layer_step(x_ref, rope_ref, dst_ref, dkt_ref, dvt_ref, y_ref, st_ref, ko_ref, vo_ref, dsto_ref, dkto_ref, dvto_ref,
                *, ng_ref, win_hbm, rng_ref, qg_ref, kg_ref, sink_ref, wbr_hbm, wbs_hbm, wo_hbm,
                dmask_ref, cross_ref, kdec_ref, gmat_ref, xs_hbm, drcos_ref, drsin_ref, dscos_ref, dssin_ref, ys_ref,
                pk_scr, pv_scr, or_scr, os_scr,
                drq_scr, drk_scr, drv_scr, dsq_scr, dsk_scr, dsv_scr, dor_scr, dos_scr,
                win_ref, wbr_ref, wbs_ref, wo_ref, wstage, wsem, xs_ref, xsem, decay_chunk, decay_step):
    t = pl.program_id(1)
    step = pl.program_id(0) * pl.num_programs(1) + t
    last_step = pl.num_programs(0) * pl.num_programs(1) - 1
    tm = x_ref.shape[0]
    nblk = tm // WINDOW
    nseq = dst_ref.shape[0]

    gmat = gmat_ref[...]
    qg = jnp.concatenate([qg_ref[...]] * (MXU_N // SWA_HD), axis=1)
    kg = jnp.concatenate([kg_ref[...]] * (MXU_N // SWA_HD), axis=1)
    lane = lax.broadcasted_iota(jnp.int32, (1, LANES), 1)
    first_half = (lane % SWA_HD) < (SWA_HD // 2)
    low = lane < SWA_HD

    @pl.when(step == 0)
    def _():
        fetch = pltpu.make_async_copy(xs_hbm.at[:, 0, :], xs_ref, xsem.at[0])
        fetch.start()
        _load_weights_bf16((win_hbm, wbr_hbm, wbs_hbm, wo_hbm), (win_ref, wbr_ref, wbs_ref, wo_ref), wstage, wsem)
        fetch.wait()

    def dproj(lo, hi):
        hs = _rms_rows(xs_ref[...], ng_ref[...]).astype(BF16)
        return _dot(hs, win_ref[:, lo:hi])

    @pl.when(step == 0)
    def _():
        drcos = drcos_ref[...]
        drsin = drsin_ref[...]
        rq = dproj(C_RQ, C_RK)
        rk = dproj(C_RK, C_RV)
        for hd in range(RET_HEADS):
            sl = slice(hd * RET_DK, (hd + 1) * RET_DK)
            drq_scr[:, sl] = _rope128(rq[:, sl], drcos, drsin)
            drk_scr[:, sl] = _rope128(rk[:, sl], drcos, drsin) * (RET_DK ** -0.5)
        drv_scr[...] = dproj(C_RV, C_RG)
        dscos = dscos_ref[...]
        dssin = dssin_ref[...]
        for cb in range(SWA_Q // MXU_N):
            dsq_scr[:, cb * MXU_N:(cb + 1) * MXU_N] = _swa_norm_rope(
                dproj(C_SQ + cb * MXU_N, C_SQ + (cb + 1) * MXU_N), gmat, qg, dscos, dssin,
                first_half) * (SWA_HD ** -0.5)
        dsk_scr[...] = _swa_norm_rope(dproj(C_SK, C_SV), gmat, kg, dscos, dssin, first_half)
        dsv_scr[...] = dproj(C_SV, C_SG)

    @pl.when(t == 0)
    def _():
        st_ref[...] = jnp.zeros_like(st_ref)
        pk_scr[...] = jnp.zeros_like(pk_scr)
        pv_scr[...] = jnp.zeros_like(pv_scr)

    seq0 = step * nseq
    grp = pl.multiple_of((seq0 // SUBLANES) * SUBLANES, SUBLANES)
    rows8 = pl.ds(grp, SUBLANES)
    row8 = lax.broadcasted_iota(jnp.int32, (SUBLANES, 1), 0)
    newest = lane == WINDOW - 1
    dstate = {}

    def column_broadcast(rows_scr):
        rowi = lax.broadcasted_iota(jnp.int32, (SUBLANES, SWA_KV), 0)
        ones_row = lax.broadcasted_iota(jnp.int32, (SUBLANES, nseq * LANES), 0)
        ones_lane = lax.broadcasted_iota(jnp.int32, (SUBLANES, nseq * LANES), 1)
        tile = jnp.zeros((SUBLANES, SWA_KV), F32)
        for s in range(nseq):
            r = rows_scr[pl.ds(seq0 + s, 1), :]
            hi = r.astype(BF16).astype(F32)
            mid = (r - hi).astype(BF16).astype(F32)
            lo = (r - hi) - mid
            for i, term in enumerate((hi, mid, lo)):
                tile = jnp.where(rowi == 3 * s + i, jnp.broadcast_to(term, (SUBLANES, SWA_KV)), tile)
        pick = (ones_row // 3 == ones_lane // LANES) & (ones_row < 3 * nseq)
        return _dot_tn(tile.astype(BF16), jnp.where(pick, 1.0, 0.0).astype(BF16))

    def decode_scores():
        q8 = drq_scr[rows8, :]
        k8 = drk_scr[rows8, :]
        v8 = drv_scr[rows8, :]
        mine = (row8 >= seq0 - grp) & (row8 < seq0 - grp + nseq)
        o_acc = [jnp.zeros((SUBLANES, RET_DV), F32)] * RET_HEADS
        knew = column_broadcast(dsk_scr)
        vnew = column_broadcast(dsv_scr)
        scs, vts = [], []
        for s in range(nseq):
            b = seq0 + s
            sel = row8 == (b - grp)
            for hd in range(RET_HEADS):
                qh = q8[:, hd * RET_DK:(hd + 1) * RET_DK]
                kh = k8[:, hd * RET_DK:(hd + 1) * RET_DK]
                vh = v8[:, hd * RET_DV:(hd + 1) * RET_DV]
                state = dst_ref[s, hd]
                inter = _dot(qh.astype(BF16), state.astype(BF16))
                outer = _dot_tn(jnp.where(sel, kh, 0.0).astype(BF16), vh.astype(BF16))
                dsto_ref[s, hd] = decay_step[hd] * state + outer
                o_acc[hd] = jnp.where(sel, inter, o_acc[hd])
            kt_new = jnp.where(newest, knew[:, s * LANES:(s + 1) * LANES], pltpu.roll(dkt_ref[s], WINDOW - 1, 1))
            vt_new = jnp.where(newest, vnew[:, s * LANES:(s + 1) * LANES], pltpu.roll(dvt_ref[s], WINDOW - 1, 1))
            dkto_ref[s] = kt_new
            dvto_ref[s] = vt_new
            qexp = _heads_to_rows(dsq_scr[pl.ds(b, 1), :]).astype(BF16)
            scs.append(_dot(qexp, kt_new.astype(BF16)))
            vts.append(vt_new.astype(BF16))
        for hd in range(RET_HEADS):
            qh = q8[:, hd * RET_DK:(hd + 1) * RET_DK]
            kh = k8[:, hd * RET_DK:(hd + 1) * RET_DK]
            vh = v8[:, hd * RET_DV:(hd + 1) * RET_DV]
            sl = slice(hd * RET_DV, (hd + 1) * RET_DV)
            o = jnp.sum(qh * kh, axis=-1, keepdims=True) * vh + decay_step[hd] * o_acc[hd]
            dor_scr[rows8, sl] = jnp.where(mine, o, dor_scr[rows8, sl])
        dstate["scs"], dstate["vts"] = scs, vts

    def decode_attend():
        head = lax.broadcasted_iota(jnp.int32, (SWA_HEADS, 1), 0)
        sink_col = jnp.zeros((SWA_HEADS, 1), F32)
        for hh in range(SWA_HEADS):
            sink_col = jnp.where(head == hh, sink_ref[hh], sink_col)
        outs = []
        for s in range(nseq):
            sc = dstate["scs"][s]
            m = jnp.maximum(jnp.max(sc, axis=-1, keepdims=True), sink_col)
            e = jnp.exp(sc - m)
            p = e / (jnp.sum(e, axis=-1, keepdims=True) + jnp.exp(sink_col - m))
            pt = jnp.concatenate([p.astype(BF16), jnp.zeros((LANES - SWA_HEADS, WINDOW), BF16)], axis=0)
            out_t = _dot_nt(dstate["vts"][s], pt)
            outs.append(out_t.T[:SWA_HEADS])
        dstate["outs"] = outs

    def decode_store():
        for s in range(nseq):
            dos_scr[pl.ds(seq0 + s, 1), :] = _rows_to_heads(dstate["outs"][s])

    x = x_ref[...]
    h = _rms_rows(x, ng_ref[...]).astype(BF16)

    def proj(lo, hi):
        return _dot(h, win_ref[:, lo:hi])

    rcos, rsin, scos, ssin = (rope_ref[:, i * LANES:(i + 1) * LANES] for i in range(4))

    rq = proj(C_RQ, C_RK)
    decode_scores()
    rk = proj(C_RK, C_RV)
    rv = proj(C_RV, C_RG).astype(BF16)
    decode_attend()
    k_scale = RET_DK ** -0.5
    scores, inter, vs = [], [], []
    for hd in range(RET_HEADS):
        q = _rope128(rq[:, hd * RET_DK:(hd + 1) * RET_DK], rcos, rsin)
        k = _rope128(rk[:, hd * RET_DK:(hd + 1) * RET_DK], rcos, rsin) * k_scale
        v = rv[:, hd * RET_DV:(hd + 1) * RET_DV]
        state = st_ref[hd]
        scores.append(_dot_nt(q.astype(BF16), k.astype(BF16)))
        inter.append(_dot((q * cross_ref[hd]).astype(BF16), state.astype(BF16)))
        st_ref[hd] = decay_chunk[hd] * state + _dot_tn((k * kdec_ref[hd]).astype(BF16), v)
        vs.append(v)
    decode_store()
    sq_cols = []
    sk = sv = None
    for hd in range(RET_HEADS):
        g = proj(C_RG + hd * RET_DV, C_RG + (hd + 1) * RET_DV)
        sq_cols.append(_swa_norm_rope(proj(C_SQ + hd * MXU_N, C_SQ + (hd + 1) * MXU_N), gmat, qg, scos, ssin,
                                      first_half) * (SWA_HD ** -0.5))
        if hd == 0:
            sk = _swa_norm_rope(proj(C_SK, C_SV), gmat, kg, scos, ssin, first_half)
        if hd == 1:
            sv = proj(C_SV, C_SG)
        o = _dot((scores[hd] * dmask_ref[hd]).astype(BF16), vs[hd]) + inter[hd]
        on = _rms_rows(o, rng_ref[hd:hd + 1, :])
        or_scr[:, hd * RET_DV:(hd + 1) * RET_DV] = (on * (g * _sigmoid(g))).astype(BF16)

    row_i = lax.broadcasted_iota(jnp.int32, (WINDOW, 2 * WINDOW), 0)
    col_j = lax.broadcasted_iota(jnp.int32, (WINDOW, 2 * WINDOW), 1)
    allowed, kbds, vbds = [], {}, {}
    for c in range(nblk):
        lower = row_i + 1
        if c == 0:
            lower = jnp.maximum(lower, jnp.where(t == 0, WINDOW, 0))
        allowed.append((col_j >= lower) & (col_j <= row_i + WINDOW))

    def block_diag(c, j, src, prev_scr):
        r0 = c * WINDOW
        prev = prev_scr[...] if c == 0 else src[r0 - WINDOW:r0]
        cat = jnp.concatenate([prev, src[r0:r0 + WINDOW]], axis=0)
        col, half = divmod(j, 2)
        a = cat[:, col * LANES:(col + 1) * LANES]
        ar = pltpu.roll(a, SWA_HD, 1)
        if half == 0:
            bd = jnp.concatenate([jnp.where(low, a, 0.0), jnp.where(low, 0.0, ar)], axis=0)
        else:
            bd = jnp.concatenate([jnp.where(low, ar, 0.0), jnp.where(low, 0.0, a)], axis=0)
        return bd.astype(BF16)

    quads = [(c, j) for j in range(SWA_KV_HEADS) for c in range(nblk)]

    def qk(i):
        c, j = quads[i]
        kbds[(c, j)] = block_diag(c, j, sk, pk_scr)
        vbds[(c, j)] = block_diag(c, j, sv, pv_scr)
        qp = jnp.concatenate([sq_cols[j][c * WINDOW:(c + 1) * WINDOW, r * LANES:(r + 1) * LANES] for r in range(2)],
                             axis=0).astype(BF16)
        return _dot_nt(qp, kbds[(c, j)])

    chunks = {}

    def filler(i):
        kind, cb = divmod(i, 4)
        if kind == 0:
            chunks["sg", cb] = proj(C_SG + cb * MXU_N, C_SG + (cb + 1) * MXU_N)
        elif kind == 1:
            chunks["mr", cb] = _sigmoid(proj(C_MR + cb * MXU_N, C_MR + (cb + 1) * MXU_N))
        elif kind == 2:
            chunks["ms", cb] = _sigmoid(proj(C_MS + cb * MXU_N, C_MS + (cb + 1) * MXU_N))
        else:
            chunks["br", cb] = _dot(or_scr[...], wbr_ref[:, cb * MXU_N:(cb + 1) * MXU_N])

    order = [0, 4, 8, 12, 1, 5, 9, 13, 2, 6, 10, 14, 3, 7, 11, 15]
    per_unit = len(order) // len(quads)
    s4_next = qk(0)
    for i, (c, j) in enumerate(quads):
        s4 = s4_next
        if i + 1 < len(quads):
            s4_next = qk(i + 1)
        for f in order[i * per_unit:(i + 1) * per_unit]:
            filler(f)
        e_rows, inv_rows = [], []
        for r in range(2):
            es, invs = [], []
            for a in range(2):
                sink = sink_ref[4 * j + 2 * r + a]
                s = jnp.where(allowed[c], s4[r * WINDOW:(r + 1) * WINDOW, a * 2 * WINDOW:(a + 1) * 2 * WINDOW], -jnp.inf)
                m = jnp.maximum(jnp.max(s, axis=-1, keepdims=True), sink)
                e = jnp.exp(s - m)
                invs.append(1.0 / (jnp.sum(e, axis=-1, keepdims=True) + jnp.exp(sink - m)))
                es.append(e.astype(BF16))
            e_rows.append(jnp.concatenate(es, axis=1))
            inv_rows.append(jnp.where(low, invs[0], invs[1]))
        o4 = _dot(jnp.concatenate(e_rows, axis=0), vbds[(c, j)])
        for r in range(2):
            p = 2 * j + r
            o2 = o4[r * WINDOW:(r + 1) * WINDOW] * inv_rows[r]
            g = chunks["sg", j][c * WINDOW:(c + 1) * WINDOW, r * LANES:(r + 1) * LANES]
            os_scr[c * WINDOW:(c + 1) * WINDOW, p * LANES:(p + 1) * LANES] = (o2 * (g * _sigmoid(g))).astype(BF16)
    pk_scr[...] = sk[tm - WINDOW:]
    pv_scr[...] = sv[tm - WINDOW:]

    merged = []
    for cb in range(D_MODEL // MXU_N):
        br_s = _dot(os_scr[...], wbs_ref[:, cb * MXU_N:(cb + 1) * MXU_N])
        merged.append((chunks["mr", cb] * chunks["br", cb] + chunks["ms", cb] * br_s).astype(BF16))
    y_ref[...] = x + _dot(jnp.concatenate(merged, axis=1), wo_ref[...])

    @pl.when(t == pl.num_programs(1) - 1)
    def _():
        ko_ref[...] = sk[tm - WINDOW:].T
        vo_ref[...] = sv[tm - WINDOW:].T

    @pl.when(step == last_step)
    def _():
        rg = dproj(C_RG, C_SQ)
        o_r = dor_scr[...]
        parts = []
        for hd in range(RET_HEADS):
            sl = slice(hd * RET_DV, (hd + 1) * RET_DV)
            parts.append(_rms_rows(o_r[:, sl], rng_ref[hd:hd + 1, :]) * (rg[:, sl] * _sigmoid(rg[:, sl])))
        br_r = _dot(jnp.concatenate(parts, axis=1).astype(BF16), wbr_ref[...])
        sg = dproj(C_SG, C_MR)
        br_s = _dot((dos_scr[...] * (sg * _sigmoid(sg))).astype(BF16), wbs_ref[...])
        mrg = _sigmoid(dproj(C_MR, C_MS)) * br_r + _sigmoid(dproj(C_MS, IN_WIDTH)) * br_s
        ys_ref[:, 0, :] = xs_ref[...] + _dot(mrg.astype(BF16), wo_ref[...])


def _rope_tables(pos, d):
    inv = np.float64(ROPE_THETA) ** (-np.arange(0, d, 2, dtype=np.float64) / d)
    ang = pos.astype(np.float64)[:, None] * inv[None, :]
    c, s = np.cos(ang), np.sin(ang)
    reps = LANES // d
    cos = np.tile(np.concatenate([c, c], axis=1), (1, reps))
    sin = np.tile(np.concatenate([-s, s], axis=1), (1, reps))
    return cos.astype(np.float32), sin.astype(np.float32)


def _log_decay():
    return np.log(1.0 - 2.0 ** (-5.0 - np.arange(RET_HEADS, dtype=np.float64)))


def _group_matrix():
    g = np.arange(MXU_N) // SWA_HD
    return jnp.asarray((g[:, None] == g[None, :]).astype(np.float32) / SWA_HD, dtype=BF16)


def _fused_layer(x, xs, state, cache_kt, cache_vt, w, tm):
    B, T, D = x.shape
    nt = T // tm
    nsteps = B * nt
    nb = xs.shape[0]
    assert nb % nsteps == 0 and nb % SUBLANES == 0
    nseq = nb // nsteps
    assert SUBLANES % nseq == 0 and 3 * nseq <= SUBLANES

    lg = _log_decay()
    n = np.arange(tm, dtype=np.float64)
    diff = n[:, None] - n[None, :]
    dmask = np.where(diff[None] >= 0, np.exp(np.maximum(diff, 0.0)[None] * lg[:, None, None]), 0.0).astype(np.float32)
    cross = np.broadcast_to(np.exp((n[None, :] + 1.0) * lg[:, None])[:, :, None], (RET_HEADS, tm, RET_DK))
    kdec = np.broadcast_to(np.exp((tm - 1.0 - n)[None, :] * lg[:, None])[:, :, None], (RET_HEADS, tm, RET_DK))
    decay_chunk = tuple(float(v) for v in np.exp(tm * lg))
    decay_step = tuple(float(v) for v in np.exp(lg))
    pos = np.arange(T)
    rcos, rsin = _rope_tables(pos, RET_DK)
    scos, ssin = _rope_tables(pos, SWA_HD)
    dpos = np.full((1,), PAST_LEN)
    drcos, drsin = _rope_tables(dpos, RET_DK)
    dscos, dssin = _rope_tables(dpos, SWA_HD)

    tok = lambda b, t: (b, t, 0)
    tab = lambda b, t: (t, 0)
    seq4 = lambda b, t: (b * nt + t, 0, 0, 0)
    seq3 = lambda b, t: (b * nt + t, 0, 0)
    step_in_specs = [
        pl.BlockSpec((None, tm, D), tok),
        pl.BlockSpec((tm, 4 * LANES), tab),
        pl.BlockSpec((nseq, RET_HEADS, RET_DK, RET_DV), seq4),
        pl.BlockSpec((nseq, SWA_KV, WINDOW), seq3),
        pl.BlockSpec((nseq, SWA_KV, WINDOW), seq3),
    ]
    step_out_specs = [
        pl.BlockSpec((None, tm, D), tok),
        pl.BlockSpec((None, RET_HEADS, RET_DK, RET_DV), lambda b, t: (b, 0, 0, 0)),
        pl.BlockSpec((None, SWA_KV, WINDOW), lambda b, t: (b, 0, 0)),
        pl.BlockSpec((None, SWA_KV, WINDOW), lambda b, t: (b, 0, 0)),
        pl.BlockSpec((nseq, RET_HEADS, RET_DK, RET_DV), seq4),
        pl.BlockSpec((nseq, SWA_KV, WINDOW), seq3),
        pl.BlockSpec((nseq, SWA_KV, WINDOW), seq3),
    ]
    hbm = pl.BlockSpec(memory_space=pltpu.HBM)
    vmem = pl.BlockSpec(memory_space=pltpu.VMEM)
    in_specs = [hbm] * 5 + [vmem, hbm, vmem, vmem, vmem, pl.BlockSpec(memory_space=pltpu.SMEM), hbm, hbm, hbm,
                            vmem, vmem, vmem, vmem, hbm, vmem, vmem, vmem, vmem]
    assert len(in_specs) == 5 + len(RESIDENT)
    out_shape = (
        jax.ShapeDtypeStruct((B, T, D), F32),
        jax.ShapeDtypeStruct((B, RET_HEADS, RET_DK, RET_DV), F32),
        jax.ShapeDtypeStruct((B, SWA_KV, WINDOW), F32),
        jax.ShapeDtypeStruct((B, SWA_KV, WINDOW), F32),
        jax.ShapeDtypeStruct((nb, 1, D), F32),
        jax.ShapeDtypeStruct(state.shape, F32),
        jax.ShapeDtypeStruct(cache_kt.shape, F32),
        jax.ShapeDtypeStruct(cache_vt.shape, F32),
    )
    out_specs = (hbm, hbm, hbm, hbm, vmem, hbm, hbm, hbm)
    scratch = [
        pltpu.VMEM((WINDOW, SWA_KV), F32),
        pltpu.VMEM((WINDOW, SWA_KV), F32),
        pltpu.VMEM((tm, RET_V), BF16),
        pltpu.VMEM((tm, SWA_Q), BF16),
        pltpu.VMEM((nb, RET_QK), F32),
        pltpu.VMEM((nb, RET_QK), F32),
        pltpu.VMEM((nb, RET_V), F32),
        pltpu.VMEM((nb, SWA_Q), F32),
        pltpu.VMEM((nb, SWA_KV), F32),
        pltpu.VMEM((nb, SWA_KV), F32),
        pltpu.VMEM((nb, RET_V), F32),
        pltpu.VMEM((nb, SWA_Q), F32),
        pltpu.VMEM((D, IN_WIDTH), BF16),
        pltpu.VMEM((RET_V, D), BF16),
        pltpu.VMEM((SWA_Q, D), BF16),
        pltpu.VMEM((D, D), BF16),
        pltpu.VMEM((2, D, WCHUNK), F32),
        pltpu.SemaphoreType.DMA((2,)),
        pltpu.VMEM((nb, D), F32),
        pltpu.SemaphoreType.DMA((1,)),
    ]
    return pl.pallas_call(
        functools.partial(_layer_kernel, grid=(B, nt), in_specs=step_in_specs, out_specs=step_out_specs,
                          decay_chunk=decay_chunk, decay_step=decay_step),
        in_specs=in_specs,
        out_specs=out_specs,
        out_shape=out_shape,
        scratch_shapes=scratch,
        compiler_params=pltpu.CompilerParams(vmem_limit_bytes=VMEM_LIMIT),
        name="hybrid_layer",
    )(x, np.concatenate([rcos, rsin, scos, ssin], axis=1), state, cache_kt, cache_vt,
      w["ng"], w["win"], w["rng"], w["qg"], w["kg"], w["sinks"], w["wbr"], w["wbs"], w["wo"],
      dmask, cross.astype(np.float32), kdec.astype(np.float32), _group_matrix(),
      xs, drcos, drsin, dscos, dssin)


def kernel(x_prompt, x_sample, state_ret, cache_swa_k, cache_swa_v, norm_g, w_in, ret_norm_g, swa_q_g, swa_k_g,
           swa_sinks, w_br_ret, w_br_swa, w_out):
    depth = norm_g.shape[0]
    assert depth == 1 and x_sample.shape[1] == 1
    B, T, D = x_prompt.shape
    nb = x_sample.shape[0]
    wc = cache_swa_k.shape[2]
    assert wc == WINDOW and T % PROMPT_TM == 0
    l = 0
    w = {
        "ng": norm_g[l][None, :],
        "win": w_in[l],
        "rng": ret_norm_g[l],
        "qg": swa_q_g[l][None, :],
        "kg": swa_k_g[l][None, :],
        "sinks": swa_sinks[l],
        "wbr": w_br_ret[l],
        "wbs": w_br_swa[l],
        "wo": w_out[l],
    }
    to_t = lambda c: jnp.transpose(c.reshape(nb, wc, SWA_KV), (0, 2, 1))
    from_t = lambda c: jnp.transpose(c, (0, 2, 1)).reshape(1, c.shape[0], wc, SWA_KV_HEADS, SWA_HD)
    yp, rp, kp, vp, ys, rs, ks, vs = _fused_layer(
        x_prompt, x_sample, state_ret[l], to_t(cache_swa_k[l]), to_t(cache_swa_v[l]), w, PROMPT_TM)
    return (yp, ys, rp[None], rs[None],
            from_t(kp), from_t(vp), from_t(ks), from_t(vs))
```
